```python
import jax
import jax.numpy as jnp
from jax import lax
import numpy as np


D_MODEL = 1024
BATCH = 4
SEQ = 4096
DEPTH = 1

D_MIX = D_MODEL
RMS_EPS = 1e-6
MLSTM_HEADS = 4
MLSTM_DQK = 128
MLSTM_DV = 128
MLSTM_CHUNK = 64
MLA_HEADS = 4
MLA_Q_RANK = 256
MLA_KV_RANK = 128
MLA_NOPE = 128
MLA_ROPE = 64
MLA_DV = 128
MLA_DQK = MLA_NOPE + MLA_ROPE
ATTN_Q_BLOCK = 128
ROPE_THETA = 10000.0
PEER_HEADS = 8
PEER_N_KEYS = 128
PEER_N_EXPERTS = PEER_N_KEYS * PEER_N_KEYS
PEER_D_KEY = 256
PEER_TOPK = 16
PEER_TOKEN_BLOCK = 128
IN_SPLITS = (
    MLSTM_HEADS * MLSTM_DQK,
    MLSTM_HEADS * MLSTM_DQK,
    MLSTM_HEADS * MLSTM_DV,
    MLSTM_HEADS * MLSTM_DV,
    MLSTM_HEADS,
    MLSTM_HEADS,
    MLA_Q_RANK,
    MLA_KV_RANK,
    MLA_ROPE,
)
D_IN = sum(IN_SPLITS)

kernel_name = 'hybrid_mlstm_mla_peer_layer'


def rms_norm(x, g):
    xf = x.astype(jnp.float32)
    y = xf * lax.rsqrt(jnp.mean(xf * xf, axis=-1, keepdims=True) + RMS_EPS)
    return (y * g.astype(jnp.float32)).astype(x.dtype)


def mlstm_group(mq, mk, mv, mo, mi, mf, b_i, b_f, norm_g):
    B, S, _ = mq.shape
    H, L = MLSTM_HEADS, MLSTM_CHUNK
    NC = S // L
    f32 = jnp.float32

    def heads(t, d):
        return t.reshape(B, S, H, d).transpose(0, 2, 1, 3).astype(f32)

    q = heads(mq, MLSTM_DQK)
    k = heads(mk, MLSTM_DQK) * (MLSTM_DQK ** -0.5)
    v = heads(mv, MLSTM_DV)
    i_pre = jnp.moveaxis((mi + b_i).astype(f32), -1, 1)
    log_f = jax.nn.log_sigmoid(jnp.moveaxis((mf + b_f).astype(f32), -1, 1))

    def to_chunks(t):
        return jnp.moveaxis(t.reshape(B, H, NC, L, *t.shape[3:]), 2, 0)

    causal = jnp.tril(jnp.ones((L, L), dtype=bool))

    def chunk_step(carry, xs):
        C, n, m = carry
        qc, kc, vc, ic, fc = xs
        b = jnp.cumsum(fc, axis=-1)
        a = b + m[..., None]
        D = b[..., :, None] - b[..., None, :] + ic[..., None, :]
        D = jnp.where(causal, D, -jnp.inf)
        m_t = jnp.maximum(a, jnp.max(D, axis=-1))
        Dw = jnp.exp(D - m_t[..., None])
        aw = jnp.exp(a - m_t)
        s = jnp.einsum('bhtd,bhsd->bhts', qc, kc) * Dw
        num = aw[..., None] * jnp.einsum('bhtd,bhde->bhte', qc, C) + jnp.einsum('bhts,bhse->bhte', s, vc)
        den = aw * jnp.einsum('bhtd,bhd->bht', qc, n) + jnp.sum(s, axis=-1)
        h = num / jnp.maximum(jnp.abs(den), jnp.exp(-m_t))[..., None]
        bL = b[..., -1]
        g = bL[..., None] - b + ic
        m_new = jnp.maximum(bL + m, jnp.max(g, axis=-1))
        decay = jnp.exp(bL + m - m_new)
        w = jnp.exp(g - m_new[..., None])
        C_new = decay[..., None, None] * C + jnp.einsum('bhs,bhsd,bhse->bhde', w, kc, vc)
        n_new = decay[..., None] * n + jnp.einsum('bhs,bhsd->bhd', w, kc)
        return (C_new, n_new, m_new), h

    init = (jnp.zeros((B, H, MLSTM_DQK, MLSTM_DV), f32),
            jnp.zeros((B, H, MLSTM_DQK), f32),
            jnp.zeros((B, H), f32))
    xs = (to_chunks(q), to_chunks(k), to_chunks(v), to_chunks(i_pre), to_chunks(log_f))
    _, h = lax.scan(chunk_step, init, xs)
    h = jnp.moveaxis(h, 0, 2).reshape(B, H, S, MLSTM_DV).transpose(0, 2, 1, 3)
    h = rms_norm(h, norm_g.reshape(H, MLSTM_DV)).reshape(B, S, H * MLSTM_DV)
    return (jax.nn.sigmoid(mo.astype(f32)) * h).astype(mq.dtype)


def apply_rope(t, cos, sin):
    nope, rp = t[..., :MLA_NOPE], t[..., MLA_NOPE:]
    half = MLA_ROPE // 2
    r1, r2 = rp[..., :half], rp[..., half:]
    rot = jnp.concatenate([r1 * cos - r2 * sin, r1 * sin + r2 * cos], axis=-1)
    return jnp.concatenate([nope, rot], axis=-1)


def mla_group(cq, ckv, kr, positions, cq_g, w_uq, ckv_g, w_ukv, q_g, k_g):
    B, S, _ = cq.shape
    H = MLA_HEADS
    f32 = jnp.float32
    q = (rms_norm(cq, cq_g) @ w_uq).reshape(B, S, H, MLA_DQK)
    kv = (rms_norm(ckv, ckv_g) @ w_ukv).reshape(B, S, H, MLA_NOPE + MLA_DV)
    k_nope, v = kv[..., :MLA_NOPE], kv[..., MLA_NOPE:]
    k_rope = jnp.broadcast_to(kr[:, :, None, :], (B, S, H, MLA_ROPE))
    k = jnp.concatenate([k_nope, k_rope], axis=-1)
    q = rms_norm(q, q_g)
    k = rms_norm(k, k_g)
    inv_freq = ROPE_THETA ** (-jnp.arange(0, MLA_ROPE, 2, dtype=f32) / MLA_ROPE)
    ang = positions.astype(f32)[..., None] * inv_freq
    cos = jnp.cos(ang)[:, :, None, :].astype(q.dtype)
    sin = jnp.sin(ang)[:, :, None, :].astype(q.dtype)
    q = apply_rope(q, cos, sin).transpose(0, 2, 1, 3)
    k = apply_rope(k, cos, sin).transpose(0, 2, 1, 3)
    v = v.transpose(0, 2, 1, 3)
    QB = ATTN_Q_BLOCK
    NB = S // QB
    scale = MLA_DQK ** -0.5
    key_pos = jnp.arange(S)

    def attend(args):
        q_blk, bi = args
        sc = jnp.einsum('bhqd,bhkd->bhqk', q_blk, k).astype(f32) * scale
        q_pos = bi * QB + jnp.arange(QB)
        sc = jnp.where(key_pos[None, :] <= q_pos[:, None], sc, -jnp.inf)
        p = jax.nn.softmax(sc, axis=-1).astype(v.dtype)
        return jnp.einsum('bhqk,bhkd->bhqd', p, v)

    q_blocks = jnp.moveaxis(q.reshape(B, H, NB, QB, MLA_DQK), 2, 0)
    o = lax.map(attend, (q_blocks, jnp.arange(NB)))
    o = jnp.moveaxis(o, 0, 2).reshape(B, H, S, MLA_DV).transpose(0, 2, 1, 3)
    return o.reshape(B, S, H * MLA_DV)


def peer(h, w_q, keys_1, keys_2, u_table, v_table):
    B, S, D = h.shape
    T = B * S
    TB = PEER_TOKEN_BLOCK
    K = PEER_TOPK
    half = PEER_D_KEY // 2

    def block(xb):
        qry = (xb @ w_q).reshape(TB, PEER_HEADS, PEER_D_KEY)
        s1 = jnp.einsum('thd,nd->thn', qry[..., :half], keys_1)
        s2 = jnp.einsum('thd,nd->thn', qry[..., half:], keys_2)
        v1, i1 = lax.top_k(s1, K)
        v2, i2 = lax.top_k(s2, K)
        cand = (v1[..., :, None] + v2[..., None, :]).reshape(TB, PEER_HEADS, K * K)
        sc, ci = lax.top_k(cand, K)
        idx = (jnp.take_along_axis(i1, ci // K, axis=-1) * PEER_N_KEYS
               + jnp.take_along_axis(i2, ci % K, axis=-1))
        gate = jax.nn.softmax(sc.astype(jnp.float32), axis=-1)
        act = jax.nn.gelu(jnp.einsum('thed,td->the', u_table[idx], xb).astype(jnp.float32), approximate=False)
        return jnp.einsum('the,thed->td', (gate * act).astype(xb.dtype), v_table[idx])

    y = lax.map(block, h.reshape(T // TB, TB, D))
    return y.reshape(B, S, D).astype(h.dtype)


def setup_inputs(seed: int = 0) -> dict:
    key = jax.random.key(seed)
    ks = jax.random.split(key, 24)
    f32 = jnp.float32

    def normal(k, shape, scale):
        return jax.random.normal(k, shape, f32) * scale

    def gain(k, shape):
        return 1.0 + 0.02 * jax.random.normal(k, shape, f32)

    x = jax.random.normal(ks[0], (BATCH, SEQ, D_MODEL), f32)
    offsets = jax.random.randint(ks[1], (BATCH, 1), 0, 1024, dtype=jnp.int32)
    positions = (offsets + jnp.arange(SEQ, dtype=jnp.int32)[None, :]).astype(jnp.int32)
    return {
        'x': x,
        'positions': positions,
        'attn_norm_g': gain(ks[2], (DEPTH, D_MODEL)),
        'w_in': normal(ks[3], (DEPTH, D_MODEL, D_IN), D_MODEL ** -0.5),
        'b_igate': normal(ks[4], (DEPTH, MLSTM_HEADS), 0.1),
        'b_fgate': jnp.linspace(3.0, 6.0, MLSTM_HEADS, dtype=f32)[None, :] + normal(ks[5], (DEPTH, MLSTM_HEADS), 0.1),
        'mlstm_norm_g': gain(ks[6], (DEPTH, MLSTM_HEADS * MLSTM_DV)),
        'cq_norm_g': gain(ks[7], (DEPTH, MLA_Q_RANK)),
        'w_uq': normal(ks[8], (DEPTH, MLA_Q_RANK, MLA_HEADS * MLA_DQK), MLA_Q_RANK ** -0.5),
        'ckv_norm_g': gain(ks[9], (DEPTH, MLA_KV_RANK)),
        'w_ukv': normal(ks[10], (DEPTH, MLA_KV_RANK, MLA_HEADS * (MLA_NOPE + MLA_DV)), MLA_KV_RANK ** -0.5),
        'q_norm_g': gain(ks[11], (DEPTH, MLA_DQK)),
        'k_norm_g': gain(ks[12], (DEPTH, MLA_DQK)),
        'w_out': normal(ks[13], (DEPTH, D_MIX, D_MODEL), D_MIX ** -0.5),
        'ffn_norm_g': gain(ks[14], (DEPTH, D_MODEL)),
        'peer_w_q': normal(ks[15], (DEPTH, D_MODEL, PEER_HEADS * PEER_D_KEY), D_MODEL ** -0.5),
        'peer_keys_1': normal(ks[16], (DEPTH, PEER_N_KEYS, PEER_D_KEY // 2), (PEER_D_KEY // 2) ** -0.5),
        'peer_keys_2': normal(ks[17], (DEPTH, PEER_N_KEYS, PEER_D_KEY // 2), (PEER_D_KEY // 2) ** -0.5),
        'peer_u': normal(ks[18], (DEPTH, PEER_N_EXPERTS, D_MODEL), D_MODEL ** -0.5),
        'peer_v': normal(ks[19], (DEPTH, PEER_N_EXPERTS, D_MODEL), PEER_HEADS ** -0.5),
    }


def reference(x, positions, attn_norm_g, w_in, b_igate, b_fgate, mlstm_norm_g, cq_norm_g, w_uq,
              ckv_norm_g, w_ukv, q_norm_g, k_norm_g, w_out, ffn_norm_g, peer_w_q, peer_keys_1,
              peer_keys_2, peer_u, peer_v):
    split_idx = [int(s) for s in np.cumsum(IN_SPLITS)[:-1]]
    for l in range(DEPTH):
        h = rms_norm(x, attn_norm_g[l])
        proj = h @ w_in[l]
        mq, mk, mv, mo, mi, mf, cq, ckv, kr = jnp.split(proj, split_idx, axis=-1)
        y_m = mlstm_group(mq, mk, mv, mo, mi, mf, b_igate[l], b_fgate[l], mlstm_norm_g[l])
        y_a = mla_group(cq, ckv, kr, positions, cq_norm_g[l], w_uq[l], ckv_norm_g[l], w_ukv[l],
                        q_norm_g[l], k_norm_g[l])
        mixed = jnp.concatenate([y_m.astype(x.dtype), y_a.astype(x.dtype)], axis=-1)
        x = x + mixed @ w_out[l]
        h = rms_norm(x, ffn_norm_g[l])
        x = x + peer(h, peer_w_q[l], peer_keys_1[l], peer_keys_2[l], peer_u[l], peer_v[l])
    return x
```

```python
import functools

import jax
import jax.numpy as jnp
from jax import lax
from jax.experimental import pallas as pl
from jax.experimental.pallas import tpu as pltpu

F32 = jnp.float32
BF16 = jnp.bfloat16

D_MODEL = 1024
RMS_EPS = 1e-6
M_H, M_DK, M_DV = 4, 128, 128
A_H, A_QR, A_KVR, A_NOPE, A_ROPE, A_DV = 4, 256, 128, 128, 64, 128
A_DQK = A_NOPE + A_ROPE
A_DPAD = 256
ROPE_THETA = 10000.0
P_H, P_NK, P_DK, P_TOPK = 8, 128, 256, 16
NEG_INF = float("-inf")

VMEM_LIMIT = 56 * 1024 * 1024


def _cparams(sem):
    return pltpu.CompilerParams(dimension_semantics=sem, vmem_limit_bytes=VMEM_LIMIT)


def _dot(a, b):
    return jnp.dot(a, b, preferred_element_type=F32)


def _dot_nt(a, b):
    return lax.dot_general(a, b, (((1,), (1,)), ((), ())), preferred_element_type=F32)


def _dot_tn(a, b):
    return lax.dot_general(a, b, (((0,), (0,)), ((), ())), preferred_element_type=F32)


def _inproj_kernel(x_ref, g_ref, wm_ref, wl_ref, wg_ref, qkvo_ref, lat_ref, gate_ref):
    x = x_ref[...]
    ms = jnp.mean(x * x, axis=-1, keepdims=True)
    h = (x * lax.rsqrt(ms + RMS_EPS) * g_ref[...]).astype(BF16)
    main = _dot(h, wm_ref[...])
    kcol = lax.broadcasted_iota(jnp.int32, (1, main.shape[1]), 1) // (M_H * M_DK) == 1
    main = jnp.where(kcol, main * (M_DK ** -0.5), main)
    qkvo_ref[...] = main.astype(BF16)
    lat_ref[...] = _dot(h, wl_ref[...])
    gate_ref[...] = _dot(h, wg_ref[...])


def _in_proj(x2, g, w_main, w_lat, w_gate, tm):
    T = x2.shape[0]
    n_main, n_lat, n_gate = w_main.shape[1], w_lat.shape[1], w_gate.shape[1]
    full = lambda shape: pl.BlockSpec(shape, lambda i: (0, 0))
    return pl.pallas_call(
        _inproj_kernel,
        grid=(T // tm,),
        in_specs=[pl.BlockSpec((tm, D_MODEL), lambda i: (i, 0)), full((1, D_MODEL)),
                  full(w_main.shape), full(w_lat.shape), full(w_gate.shape)],
        out_specs=[pl.BlockSpec((tm, n_main), lambda i: (i, 0)),
                   pl.BlockSpec((tm, n_lat), lambda i: (i, 0)),
                   pl.BlockSpec((tm, n_gate), lambda i: (i, 0))],
        out_shape=[jax.ShapeDtypeStruct((T, n_main), BF16),
                   jax.ShapeDtypeStruct((T, n_lat), F32),
                   jax.ShapeDtypeStruct((T, n_gate), F32)],
        compiler_params=_cparams(("parallel",)),
        name="in_proj",
    )(x2, g, w_main, w_lat, w_gate)


def _split3(x):
    hi = x.astype(BF16)
    r1 = x - hi.astype(F32)
    mid = r1.astype(BF16)
    lo = (r1 - mid.astype(F32)).astype(BF16)
    return hi, mid, lo


def _log_sigmoid(x):
    return -(jnp.maximum(-x, 0.0) + jnp.log1p(jnp.exp(-jnp.abs(x))))


def _mlstm_kernel(qkvo_ref, gate_ref, bias_ref, ng_ref, y_ref, c_ref, m_ref, *, L):
    c_idx = pl.program_id(1)

    @pl.when(c_idx == 0)
    def _():
        c_ref[...] = jnp.zeros_like(c_ref)
        m_ref[...] = jnp.zeros_like(m_ref)

    H, DK, DV = M_H, M_DK, M_DV
    G = gate_ref[...] + bias_ref[...]
    LF = _log_sigmoid(G)
    row = lax.broadcasted_iota(jnp.int32, (L, L), 0)
    col = lax.broadcasted_iota(jnp.int32, (L, L), 1)
    causal = col <= row
    tri = jnp.where(causal, 1.0, 0.0).astype(BF16)
    hi, mid, lo = _split3(LF)
    Bc = _dot(tri, hi) + _dot(tri, mid) + _dot(tri, lo)
    lane = lax.broadcasted_iota(jnp.int32, (L, 128), 1)
    ZT = jnp.where(lane < H, G, Bc).T
    ones_col = jnp.where(lax.broadcasted_iota(jnp.int32, (L, 128), 1) == 0, 1.0, 0.0).astype(BF16)

    for h in range(H):
        q = qkvo_ref[:, h * DK:(h + 1) * DK]
        k = qkvo_ref[:, H * DK + h * DK: H * DK + (h + 1) * DK]
        v = qkvo_ref[:, 2 * H * DK + h * DV: 2 * H * DK + (h + 1) * DV]
        o = qkvo_ref[:, 3 * H * DK + h * DV: 3 * H * DK + (h + 1) * DV]
        i_col = G[:, h:h + 1]
        b_col = Bc[:, H + h:H + h + 1]
        i_row = ZT[h:h + 1, :]
        b_row = ZT[H + h:H + h + 1, :]
        m_prev = m_ref[h]
        c_prev = c_ref[h]

        a_col = b_col + m_prev
        D = jnp.where(causal, b_col - b_row + i_row, NEG_INF)
        m_t = jnp.maximum(a_col, jnp.max(D, axis=-1, keepdims=True))
        Dw = jnp.exp(D - m_t)
        aw = jnp.exp(a_col - m_t)
        s = (_dot_nt(q, k) * Dw).astype(BF16)
        v_aug = jnp.concatenate([v, ones_col], axis=-1)
        num = aw * _dot(q, c_prev.astype(BF16)) + _dot(s, v_aug)
        den = num[:, DV:DV + 1]
        hval = num[:, :DV] / jnp.maximum(jnp.abs(den), jnp.exp(-m_t))

        bL = b_col[L - 1:L, :]
        g_col = bL - b_col + i_col
        m_new = jnp.maximum(bL + m_prev, jnp.max(g_col, axis=0, keepdims=True))
        decay = jnp.exp(bL + m_prev - m_new)
        w_col = jnp.exp(g_col - m_new)
        wv = (w_col * v_aug.astype(F32)).astype(BF16)
        c_ref[h] = decay * c_prev + _dot_tn(k, wv)
        m_ref[h] = m_new

        ms = jnp.mean(hval * hval, axis=-1, keepdims=True)
        hn = hval * lax.rsqrt(ms + RMS_EPS) * ng_ref[:, h * DV:(h + 1) * DV]
        y_ref[:, h * DV:(h + 1) * DV] = (jax.nn.sigmoid(o.astype(F32)) * hn).astype(BF16)


def _mlstm(qkvo, gates, bias_row, norm_g, B, S, L):
    T = B * S
    nc = S // L
    return pl.pallas_call(
        functools.partial(_mlstm_kernel, L=L),
        grid=(B, nc),
        in_specs=[pl.BlockSpec((L, qkvo.shape[1]), lambda b, c: (b * nc + c, 0)),
                  pl.BlockSpec((L, 128), lambda b, c: (b * nc + c, 0)),
                  pl.BlockSpec((1, 128), lambda b, c: (0, 0)),
                  pl.BlockSpec((1, M_H * M_DV), lambda b, c: (0, 0))],
        out_specs=pl.BlockSpec((L, M_H * M_DV), lambda b, c: (b * nc + c, 0)),
        out_shape=jax.ShapeDtypeStruct((T, M_H * M_DV), BF16),
        scratch_shapes=[pltpu.VMEM((M_H, M_DK, 2 * M_DV), F32),
                        pltpu.VMEM((M_H, 1, 1), F32)],
        compiler_params=_cparams(("parallel", "arbitrary")),
        name="mlstm",
    )(qkvo, gates, bias_row, norm_g)


def _mla_prep_kernel(lat_ref, pos_ref, cqg_ref, ckvg_ref, wq_ref, wqs_ref, wkv_ref,
                     qg_ref, qgs_ref, kg_ref, kgs_ref, freq_ref, sign_ref,
                     q_ref, k_ref, v_ref):
    H = A_H
    cq = lat_ref[:, :A_QR]
    ckv = lat_ref[:, A_QR:A_QR + A_KVR]
    kr = lat_ref[:, A_QR + A_KVR:A_QR + A_KVR + 128]
    krs = lat_ref[:, A_QR + A_KVR + 128:A_QR + A_KVR + 256]

    def rms(t, g):
        return t * lax.rsqrt(jnp.mean(t * t, axis=-1, keepdims=True) + RMS_EPS) * g

    cqn = rms(cq, cqg_ref[...]).astype(BF16)
    ckvn = rms(ckv, ckvg_ref[...]).astype(BF16)
    qf = _dot(cqn, wq_ref[...])
    qs = _dot(cqn, wqs_ref[...])
    kv = _dot(ckvn, wkv_ref[...])

    ang = pos_ref[...] * freq_ref[...]
    cos = jnp.cos(ang)
    sin = jnp.sin(ang) * sign_ref[...]
    scale = A_DQK ** -0.5
    kr_ss = jnp.sum(kr * kr, axis=-1, keepdims=True)

    for h in range(H):
        qh = qf[:, h * A_DPAD:(h + 1) * A_DPAD]
        rstd = lax.rsqrt(jnp.sum(qh * qh, axis=-1, keepdims=True) / A_DQK + RMS_EPS) * scale
        q_nope = qh[:, :128] * rstd * qg_ref[:, :128]
        q_rope = qh[:, 128:] * rstd * qg_ref[:, 128:]
        q_rope_s = qs[:, h * 128:(h + 1) * 128] * rstd * qgs_ref[...]
        q_ref[:, h * A_DPAD:h * A_DPAD + 128] = q_nope.astype(BF16)
        q_ref[:, h * A_DPAD + 128:(h + 1) * A_DPAD] = (q_rope * cos + q_rope_s * sin).astype(BF16)

        kn = kv[:, h * 256:h * 256 + 128]
        rstd_k = lax.rsqrt((jnp.sum(kn * kn, axis=-1, keepdims=True) + kr_ss) / A_DQK + RMS_EPS)
        k_nope = kn * rstd_k * kg_ref[:, :128]
        k_rope = kr * rstd_k * kg_ref[:, 128:]
        k_rope_s = krs * rstd_k * kgs_ref[...]
        k_ref[:, h * A_DPAD:h * A_DPAD + 128] = k_nope.astype(BF16)
        k_ref[:, h * A_DPAD + 128:(h + 1) * A_DPAD] = (k_rope * cos + k_rope_s * sin).astype(BF16)
        v_ref[:, h * A_DV:(h + 1) * A_DV] = kv[:, h * 256 + 128:(h + 1) * 256].astype(BF16)


def _mla_prep(lat, pos_col, consts, tm):
    T = lat.shape[0]
    full = lambda a: pl.BlockSpec(a.shape, lambda i: (0, 0))
    return pl.pallas_call(
        _mla_prep_kernel,
        grid=(T // tm,),
        in_specs=[pl.BlockSpec((tm, lat.shape[1]), lambda i: (i, 0)),
                  pl.BlockSpec((tm, 1), lambda i: (i, 0))] + [full(c) for c in consts],
        out_specs=[pl.BlockSpec((tm, A_H * A_DPAD), lambda i: (i, 0)),
                   pl.BlockSpec((tm, A_H * A_DPAD), lambda i: (i, 0)),
                   pl.BlockSpec((tm, A_H * A_DV), lambda i: (i, 0))],
        out_shape=[jax.ShapeDtypeStruct((T, A_H * A_DPAD), BF16),
                   jax.ShapeDtypeStruct((T, A_H * A_DPAD), BF16),
                   jax.ShapeDtypeStruct((T, A_H * A_DV), BF16)],
        compiler_params=_cparams(("parallel",)),
        name="mla_prep",
    )(lat, pos_col, *consts)


def _attn_kernel(q_ref, k_ref, v_ref, o_ref, *, tq):
    qi = pl.program_id(2)
    q = q_ref[...]

    def chunk(j, carry, masked):
        m, l, acc = carry
        k = k_ref[pl.ds(pl.multiple_of(j * tq, tq), tq), :]
        v = v_ref[pl.ds(pl.multiple_of(j * tq, tq), tq), :]
        s = _dot_nt(q, k)
        if masked:
            row = lax.broadcasted_iota(jnp.int32, (tq, tq), 0)
            col = lax.broadcasted_iota(jnp.int32, (tq, tq), 1)
            s = jnp.where(col <= row, s, NEG_INF)
        m_new = jnp.maximum(m, jnp.max(s, axis=-1, keepdims=True))
        alpha = jnp.exp(m - m_new)
        p = jnp.exp(s - m_new)
        l = alpha * l + jnp.sum(p, axis=-1, keepdims=True)
        acc = alpha * acc + _dot(p.astype(BF16), v)
        return m_new, l, acc

    init = (jnp.full((tq, 1), NEG_INF, F32), jnp.zeros((tq, 1), F32), jnp.zeros((tq, A_DV), F32))
    carry = lax.fori_loop(0, qi, lambda j, c: chunk(j, c, False), init)
    m, l, acc = chunk(qi, carry, True)
    o_ref[...] = (acc / l).astype(BF16)


def _mla_attn(q, k, v, B, S, tq):
    T = B * S
    nq = S // tq
    return pl.pallas_call(
        functools.partial(_attn_kernel, tq=tq),
        grid=(B, A_H, nq),
        in_specs=[pl.BlockSpec((tq, A_DPAD), lambda b, h, i: (b * nq + i, h)),
                  pl.BlockSpec((S, A_DPAD), lambda b, h, i: (b, h)),
                  pl.BlockSpec((S, A_DV), lambda b, h, i: (b, h))],
        out_specs=pl.BlockSpec((tq, A_DV), lambda b, h, i: (b * nq + i, h)),
        out_shape=jax.ShapeDtypeStruct((T, A_H * A_DV), BF16),
        compiler_params=_cparams(("parallel", "parallel", "arbitrary")),
        name="mla_attn",
    )(q, k, v)


def _outproj_kernel(x_ref, ym_ref, ya_ref, wom_ref, woa_ref, g_ref, wq_ref, k1_ref, k2_ref,
                    x1_ref, h2_ref, s1_ref, s2_ref):
    x1 = x_ref[...] + _dot(ym_ref[...], wom_ref[...]) + _dot(ya_ref[...], woa_ref[...])
    x1_ref[...] = x1
    ms = jnp.mean(x1 * x1, axis=-1, keepdims=True)
    h2 = (x1 * lax.rsqrt(ms + RMS_EPS) * g_ref[...]).astype(BF16)
    h2_ref[...] = h2
    qry = _dot(h2, wq_ref[...]).astype(BF16)
    half = P_DK // 2
    for h in range(P_H):
        s1_ref[h] = _dot_nt(k1_ref[...], qry[:, h * P_DK:h * P_DK + half])
        s2_ref[h] = _dot_nt(k2_ref[...], qry[:, h * P_DK + half:(h + 1) * P_DK])


def _out_proj(x2, ym, ya, wo_m, wo_a, g, wq, k1, k2, tm):
    T = x2.shape[0]
    full = lambda a: pl.BlockSpec(a.shape, lambda i: (0, 0))
    return pl.pallas_call(
        _outproj_kernel,
        grid=(T // tm,),
        in_specs=[pl.BlockSpec((tm, D_MODEL), lambda i: (i, 0)),
                  pl.BlockSpec((tm, ym.shape[1]), lambda i: (i, 0)),
                  pl.BlockSpec((tm, ya.shape[1]), lambda i: (i, 0)),
                  full(wo_m), full(wo_a), full(g), full(wq), full(k1), full(k2)],
        out_specs=[pl.BlockSpec((tm, D_MODEL), lambda i: (i, 0)),
                   pl.BlockSpec((tm, D_MODEL), lambda i: (i, 0)),
                   pl.BlockSpec((P_H, P_NK, tm), lambda i: (0, 0, i)),
                   pl.BlockSpec((P_H, P_NK, tm), lambda i: (0, 0, i))],
        out_shape=[jax.ShapeDtypeStruct((T, D_MODEL), F32),
                   jax.ShapeDtypeStruct((T, D_MODEL), BF16),
                   jax.ShapeDtypeStruct((P_H, P_NK, T), F32),
                   jax.ShapeDtypeStruct((P_H, P_NK, T), F32)],
        compiler_params=_cparams(("parallel",)),
        name="out_peerq",
    )(x2, ym, ya, wo_m, wo_a, g, wq, k1, k2)


def _top16(x):
    vals = []
    for _ in range(P_TOPK):
        m = jnp.max(x, axis=0, keepdims=True)
        vals.append(m)
        x = jnp.where(x == m, NEG_INF, x)
    return vals


_CAND = [(a, b) for a in range(P_TOPK) for b in range(P_TOPK) if (a + 1) * (b + 1) <= P_TOPK]


def _route_kernel(s1_ref, s2_ref, c_ref, f_ref, e2_ref):
    s1 = s1_ref[0]
    s2 = s2_ref[0]
    v1 = _top16(s1)
    v2 = _top16(s2)
    cand = jnp.concatenate([v1[a] + v2[b] for a, b in _CAND], axis=0)
    tau = _top16(cand)[P_TOPK - 1]
    top = v1[0] + v2[0]
    z = jnp.zeros_like(tau)
    thr = [jnp.full_like(tau, jnp.inf) for _ in range(P_TOPK)]
    for a, b in _CAND:
        sm = v1[a] + v2[b]
        sel = sm >= tau
        z = z + jnp.where(sel, jnp.exp(sm - top), 0.0)
        thr[a] = jnp.where(sel, jnp.minimum(thr[a], v2[b]), thr[a])
    c = jnp.full_like(s1, jnp.inf)
    for a in range(P_TOPK):
        c = jnp.where(s1 == v1[a], thr[a], c)
    c_ref[0] = c
    f_ref[0] = jnp.exp(s1 - v1[0]) / z
    e2_ref[0] = jnp.exp(s2 - v2[0])


def _peer_route(s1t, s2t, tl):
    T = s1t.shape[2]
    spec = pl.BlockSpec((1, P_NK, tl), lambda h, i: (h, 0, i))
    shp = jax.ShapeDtypeStruct((P_H, P_NK, T), F32)
    return pl.pallas_call(
        _route_kernel,
        grid=(P_H, T // tl),
        in_specs=[spec, spec],
        out_specs=[spec, spec, spec],
        out_shape=[shp, shp, shp],
        compiler_params=_cparams(("parallel", "parallel")),
        name="peer_route",
    )(s1t, s2t)


def _gelu(x):
    return 0.5 * x * (1.0 + lax.erf(x * (2.0 ** -0.5)))


def _peer_kernel(h2_ref, x1_ref, u_ref, v_ref, s2_ref, c_ref, f_ref, e2_ref, o_ref, acc_ref, *, nb):
    e = pl.program_id(1)

    @pl.when(e == 0)
    def _():
        acc_ref[...] = jnp.zeros_like(acc_ref)

    a_t = _dot_nt(u_ref[...], h2_ref[...])
    ps = []
    for j in range(nb):
        n1 = e * nb + j
        w = None
        for h in range(P_H):
            c_row = c_ref[h, pl.ds(n1, 1), :]
            f_row = f_ref[h, pl.ds(n1, 1), :]
            t = jnp.where(s2_ref[h] >= c_row, e2_ref[h], 0.0) * f_row
            w = t if w is None else w + t
        ps.append((w * _gelu(a_t[j * P_NK:(j + 1) * P_NK, :])).astype(BF16))
    p = jnp.concatenate(ps, axis=0)
    acc_ref[...] += _dot_tn(p, v_ref[...])

    @pl.when(e == pl.num_programs(1) - 1)
    def _():
        o_ref[...] = x1_ref[...] + acc_ref[...]


def _peer_mlp(h2, x1, u, v, s2t, ct, ft, e2t, tm, nb):
    T = h2.shape[0]
    ne = P_NK // nb
    te = nb * P_NK
    rspec = pl.BlockSpec((P_H, P_NK, tm), lambda i, e: (0, 0, i))
    return pl.pallas_call(
        functools.partial(_peer_kernel, nb=nb),
        grid=(T // tm, ne),
        in_specs=[pl.BlockSpec((tm, D_MODEL), lambda i, e: (i, 0)),
                  pl.BlockSpec((tm, D_MODEL), lambda i, e: (i, 0)),
                  pl.BlockSpec((te, D_MODEL), lambda i, e: (e, 0)),
                  pl.BlockSpec((te, D_MODEL), lambda i, e: (e, 0)),
                  rspec, rspec, rspec, rspec],
        out_specs=pl.BlockSpec((tm, D_MODEL), lambda i, e: (i, 0)),
        out_shape=jax.ShapeDtypeStruct((T, D_MODEL), F32),
        scratch_shapes=[pltpu.VMEM((tm, D_MODEL), F32)],
        compiler_params=_cparams(("parallel", "arbitrary")),
        name="peer_mlp",
    )(h2, x1, u, v, s2t, ct, ft, e2t)


def _swap_halves(t, axis=-1):
    a, b = jnp.split(t, 2, axis=axis)
    return jnp.concatenate([b, a], axis=axis)


def _pad_cols(t, n):
    return jnp.pad(t, ((0, 0), (0, n - t.shape[1])))


def _layer(x2, pos_col, B, S, attn_norm_g, w_in, b_igate, b_fgate, mlstm_norm_g, cq_norm_g, w_uq,
           ckv_norm_g, w_ukv, q_norm_g, k_norm_g, w_out, ffn_norm_g, peer_w_q, peer_keys_1,
           peer_keys_2, peer_u, peer_v):
    T = B * S
    n_qkvo = 4 * M_H * M_DK
    o_gate = n_qkvo
    o_cq = o_gate + 2 * M_H
    o_ckv = o_cq + A_QR
    o_kr = o_ckv + A_KVR

    w_main = w_in[:, :n_qkvo].astype(BF16)
    w_kr = w_in[:, o_kr:o_kr + A_ROPE]
    w_lat = jnp.concatenate([w_in[:, o_cq:o_kr], _pad_cols(w_kr, 128),
                             _pad_cols(_swap_halves(w_kr), 128)], axis=1).astype(BF16)
    w_gate = _pad_cols(w_in[:, o_gate:o_gate + 2 * M_H], 128).astype(BF16)
    bias_row = _pad_cols(jnp.concatenate([b_igate, b_fgate])[None, :], 128)

    wq_h = w_uq.reshape(A_QR, A_H, A_DQK)
    wq_pad = jnp.pad(wq_h, ((0, 0), (0, 0), (0, A_DPAD - A_DQK))).reshape(A_QR, A_H * A_DPAD).astype(BF16)
    wq_sw = jnp.pad(_swap_halves(wq_h[:, :, A_NOPE:]), ((0, 0), (0, 0), (0, 128 - A_ROPE)))
    wq_sw = wq_sw.reshape(A_QR, A_H * 128).astype(BF16)
    w_kv = w_ukv.astype(BF16)
    qg = _pad_cols(q_norm_g[None, :], A_DPAD)
    qgs = _pad_cols(_swap_halves(q_norm_g[A_NOPE:])[None, :], 128)
    kg = _pad_cols(k_norm_g[None, :], A_DPAD)
    kgs = _pad_cols(_swap_halves(k_norm_g[A_NOPE:])[None, :], 128)
    inv_freq = ROPE_THETA ** (-jnp.arange(0, A_ROPE, 2, dtype=F32) / A_ROPE)
    freq_row = _pad_cols(jnp.concatenate([inv_freq, inv_freq])[None, :], 128)
    half = A_ROPE // 2
    sign_row = _pad_cols(jnp.concatenate([-jnp.ones((half,), F32), jnp.ones((half,), F32)])[None, :], 128)

    qkvo, lat, gates = _in_proj(x2, attn_norm_g[None, :], w_main, w_lat, w_gate, tm=min(512, T))
    y_m = _mlstm(qkvo, gates, bias_row, mlstm_norm_g[None, :], B, S, L=min(256, S))
    consts = [cq_norm_g[None, :], ckv_norm_g[None, :], wq_pad, wq_sw, w_kv, qg, qgs, kg, kgs,
              freq_row, sign_row]
    q, k, v = _mla_prep(lat, pos_col, consts, tm=min(512, T))
    y_a = _mla_attn(q, k, v, B, S, tq=min(512, S))
    wo = w_out.astype(BF16)
    x1, h2, s1t, s2t = _out_proj(x2, y_m, y_a, wo[:M_H * M_DV], wo[M_H * M_DV:], ffn_norm_g[None, :],
                                 peer_w_q.astype(BF16), peer_keys_1.astype(BF16),
                                 peer_keys_2.astype(BF16), tm=min(512, T))
    ct, ft, e2t = _peer_route(s1t, s2t, tl=min(256, T))
    return _peer_mlp(h2, x1, peer_u.astype(BF16), peer_v.astype(BF16), s2t, ct, ft, e2t,
                     tm=min(512, T), nb=4)


def kernel(x, positions, attn_norm_g, w_in, b_igate, b_fgate, mlstm_norm_g, cq_norm_g, w_uq, ckv_norm_g,
           w_ukv, q_norm_g, k_norm_g, w_out, ffn_norm_g, peer_w_q, peer_keys_1, peer_keys_2, peer_u, peer_v):
    B, S, D = x.shape
    x2 = x.reshape(B * S, D)
    pos_col = positions.reshape(B * S, 1).astype(F32)
    for l in range(attn_norm_g.shape[0]):
        x2 = _layer(x2, pos_col, B, S, attn_norm_g[l], w_in[l], b_igate[l], b_fgate[l], mlstm_norm_g[l],
                    cq_norm_g[l], w_uq[l], ckv_norm_g[l], w_ukv[l], q_norm_g[l], k_norm_g[l], w_out[l],
                    ffn_norm_g[l], peer_w_q[l], peer_keys_1[l], peer_keys_2[l], peer_u[l], peer_v[l])
    return x2.reshape(B, S, D)
```

```python
import functools

import jax
import jax.numpy as jnp
from jax import lax
from jax.experimental import pallas as pl
from jax.experimental.pallas import tpu as pltpu

F32 = jnp.float32
BF16 = jnp.bfloat16

D_MODEL = 1024
RMS_EPS = 1e-6
M_H, M_DK, M_DV = 4, 128, 128
A_H, A_QR, A_KVR, A_NOPE, A_ROPE, A_DV = 4, 256, 128, 128, 64, 128
A_DQK = A_NOPE + A_ROPE
A_DPAD = 256
ROPE_THETA = 10000.0
P_H, P_NK, P_DK, P_TOPK = 8, 128, 256, 16
NEG_INF = float("-inf")

VMEM_LIMIT = 56 * 1024 * 1024


def _cparams(sem):
    return pltpu.CompilerParams(dimension_semantics=sem, vmem_limit_bytes=VMEM_LIMIT)


def _dot(a, b):
    return jnp.dot(a, b, preferred_element_type=F32)


def _dot_nt(a, b):
    return lax.dot_general(a, b, (((1,), (1,)), ((), ())), preferred_element_type=F32)


def _dot_tn(a, b):
    return lax.dot_general(a, b, (((0,), (0,)), ((), ())), preferred_element_type=F32)


def _inproj_kernel(x_ref, g_ref, wm_ref, wl_ref, wg_ref, qkvo_ref, lat_ref, gate_ref):
    x = x_ref[...]
    ms = jnp.mean(x * x, axis=-1, keepdims=True)
    h = (x * lax.rsqrt(ms + RMS_EPS) * g_ref[...]).astype(BF16)
    main = _dot(h, wm_ref[...])
    kcol = lax.broadcasted_iota(jnp.int32, (1, main.shape[1]), 1) // (M_H * M_DK) == 1
    main = jnp.where(kcol, main * (M_DK ** -0.5), main)
    qkvo_ref[...] = main.astype(BF16)
    lat_ref[...] = _dot(h, wl_ref[...])
    gate_ref[...] = _dot(h, wg_ref[...])


def _in_proj(x2, g, w_main, w_lat, w_gate, tm):
    T = x2.shape[0]
    n_main, n_lat, n_gate = w_main.shape[1], w_lat.shape[1], w_gate.shape[1]
    full = lambda shape: pl.BlockSpec(shape, lambda i: (0, 0))
    return pl.pallas_call(
        _inproj_kernel,
        grid=(T // tm,),
        in_specs=[pl.BlockSpec((tm, D_MODEL), lambda i: (i, 0)), full((1, D_MODEL)),
                  full(w_main.shape), full(w_lat.shape), full(w_gate.shape)],
        out_specs=[pl.BlockSpec((tm, n_main), lambda i: (i, 0)),
                   pl.BlockSpec((tm, n_lat), lambda i: (i, 0)),
                   pl.BlockSpec((tm, n_gate), lambda i: (i, 0))],
        out_shape=[jax.ShapeDtypeStruct((T, n_main), BF16),
                   jax.ShapeDtypeStruct((T, n_lat), F32),
                   jax.ShapeDtypeStruct((T, n_gate), F32)],
        compiler_params=_cparams(("parallel",)),
        name="in_proj",
    )(x2, g, w_main, w_lat, w_gate)


def _log_sigmoid(x):
    return -(jnp.maximum(-x, 0.0) + jnp.log1p(jnp.exp(-jnp.abs(x))))


def _mlstm_kernel(qkvo_ref, gate_ref, bias_ref, ng_ref, y_ref, c_ref, m_ref, *, L):
    c_idx = pl.program_id(1)

    @pl.when(c_idx == 0)
    def _():
        c_ref[...] = jnp.zeros_like(c_ref)
        m_ref[...] = jnp.zeros_like(m_ref)

    H, DK, DV = M_H, M_DK, M_DV
    G = gate_ref[...] + bias_ref[...]
    LF = _log_sigmoid(G)
    row = lax.broadcasted_iota(jnp.int32, (L, L), 0)
    col = lax.broadcasted_iota(jnp.int32, (L, L), 1)
    causal = col <= row
    tri = jnp.where(causal, 1.0, 0.0)
    Bc = jnp.dot(tri, LF, preferred_element_type=F32, precision=lax.Precision.HIGHEST)
    lane = lax.broadcasted_iota(jnp.int32, (L, 128), 1)
    ZT = jnp.where(lane < H, G, Bc).T
    ones_col = jnp.where(lax.broadcasted_iota(jnp.int32, (L, 128), 1) == 0, 1.0, 0.0).astype(BF16)

    for h in range(H):
        q = qkvo_ref[:, h * DK:(h + 1) * DK]
        k = qkvo_ref[:, H * DK + h * DK: H * DK + (h + 1) * DK]
        v = qkvo_ref[:, 2 * H * DK + h * DV: 2 * H * DK + (h + 1) * DV]
        o = qkvo_ref[:, 3 * H * DK + h * DV: 3 * H * DK + (h + 1) * DV]
        i_col = G[:, h:h + 1]
        b_col = Bc[:, H + h:H + h + 1]
        i_row = ZT[h:h + 1, :]
        b_row = ZT[H + h:H + h + 1, :]
        m_prev = m_ref[h]
        c_prev = c_ref[h]

        a_col = b_col + m_prev
        D = jnp.where(causal, b_col - b_row + i_row, NEG_INF)
        m_t = jnp.maximum(a_col, jnp.max(D, axis=-1, keepdims=True))
        Dw = jnp.exp(D - m_t)
        aw = jnp.exp(a_col - m_t)
        s = (_dot_nt(q, k) * Dw).astype(BF16)
        v_aug = jnp.concatenate([v, ones_col], axis=-1)
        num = aw * _dot(q, c_prev.astype(BF16)) + _dot(s, v_aug)
        den = num[:, DV:DV + 1]
        hval = num[:, :DV] / jnp.maximum(jnp.abs(den), jnp.exp(-m_t))

        bL = b_col[L - 1:L, :]
        g_col = bL - b_col + i_col
        m_new = jnp.maximum(bL + m_prev, jnp.max(g_col, axis=0, keepdims=True))
        decay = jnp.exp(bL + m_prev - m_new)
        w_col = jnp.exp(g_col - m_new)
        wv = (w_col * v_aug.astype(F32)).astype(BF16)
        c_ref[h] = decay * c_prev + _dot_tn(k, wv)
        m_ref[h] = m_new

        ms = jnp.mean(hval * hval, axis=-1, keepdims=True)
        hn = hval * lax.rsqrt(ms + RMS_EPS) * ng_ref[:, h * DV:(h + 1) * DV]
        y_ref[:, h * DV:(h + 1) * DV] = (jax.nn.sigmoid(o.astype(F32)) * hn).astype(BF16)


def _mlstm(qkvo, gates, bias_row, norm_g, B, S, L):
    T = B * S
    nc = S // L
    return pl.pallas_call(
        functools.partial(_mlstm_kernel, L=L),
        grid=(B, nc),
        in_specs=[pl.BlockSpec((L, qkvo.shape[1]), lambda b, c: (b * nc + c, 0)),
                  pl.BlockSpec((L, 128), lambda b, c: (b * nc + c, 0)),
                  pl.BlockSpec((1, 128), lambda b, c: (0, 0)),
                  pl.BlockSpec((1, M_H * M_DV), lambda b, c: (0, 0))],
        out_specs=pl.BlockSpec((L, M_H * M_DV), lambda b, c: (b * nc + c, 0)),
        out_shape=jax.ShapeDtypeStruct((T, M_H * M_DV), BF16),
        scratch_shapes=[pltpu.VMEM((M_H, M_DK, 2 * M_DV), F32),
                        pltpu.VMEM((M_H, 1, 1), F32)],
        compiler_params=_cparams(("parallel", "arbitrary")),
        name="mlstm",
    )(qkvo, gates, bias_row, norm_g)


def _mla_prep_kernel(lat_ref, pos_ref, cqg_ref, ckvg_ref, wq_ref, wqs_ref, wkv_ref,
                     qg_ref, qgs_ref, kg_ref, kgs_ref, freq_ref, sign_ref,
                     q_ref, k_ref, v_ref):
    H = A_H
    cq = lat_ref[:, :A_QR]
    ckv = lat_ref[:, A_QR:A_QR + A_KVR]
    kr = lat_ref[:, A_QR + A_KVR:A_QR + A_KVR + 128]
    krs = lat_ref[:, A_QR + A_KVR + 128:A_QR + A_KVR + 256]

    def rms(t, g):
        return t * lax.rsqrt(jnp.mean(t * t, axis=-1, keepdims=True) + RMS_EPS) * g

    cqn = rms(cq, cqg_ref[...]).astype(BF16)
    ckvn = rms(ckv, ckvg_ref[...]).astype(BF16)
    qf = _dot(cqn, wq_ref[...])
    qs = _dot(cqn, wqs_ref[...])
    kv = _dot(ckvn, wkv_ref[...])

    ang = pos_ref[...] * freq_ref[...]
    cos = jnp.cos(ang)
    sin = jnp.sin(ang) * sign_ref[...]
    scale = A_DQK ** -0.5
    kr_ss = jnp.sum(kr * kr, axis=-1, keepdims=True)

    for h in range(H):
        qh = qf[:, h * A_DPAD:(h + 1) * A_DPAD]
        rstd = lax.rsqrt(jnp.sum(qh * qh, axis=-1, keepdims=True) / A_DQK + RMS_EPS) * scale
        q_nope = qh[:, :128] * rstd * qg_ref[:, :128]
        q_rope = qh[:, 128:] * rstd * qg_ref[:, 128:]
        q_rope_s = qs[:, h * 128:(h + 1) * 128] * rstd * qgs_ref[...]
        q_ref[:, h * A_DPAD:h * A_DPAD + 128] = q_nope.astype(BF16)
        q_ref[:, h * A_DPAD + 128:(h + 1) * A_DPAD] = (q_rope * cos + q_rope_s * sin).astype(BF16)

        kn = kv[:, h * 256:h * 256 + 128]
        rstd_k = lax.rsqrt((jnp.sum(kn * kn, axis=-1, keepdims=True) + kr_ss) / A_DQK + RMS_EPS)
        k_nope = kn * rstd_k * kg_ref[:, :128]
        k_rope = kr * rstd_k * kg_ref[:, 128:]
        k_rope_s = krs * rstd_k * kgs_ref[...]
        k_ref[:, h * A_DPAD:h * A_DPAD + 128] = k_nope.astype(BF16)
        k_ref[:, h * A_DPAD + 128:(h + 1) * A_DPAD] = (k_rope * cos + k_rope_s * sin).astype(BF16)
        v_ref[:, h * A_DV:(h + 1) * A_DV] = kv[:, h * 256 + 128:(h + 1) * 256].astype(BF16)


def _mla_prep(lat, pos_col, consts, tm):
    T = lat.shape[0]
    full = lambda a: pl.BlockSpec(a.shape, lambda i: (0, 0))
    return pl.pallas_call(
        _mla_prep_kernel,
        grid=(T // tm,),
        in_specs=[pl.BlockSpec((tm, lat.shape[1]), lambda i: (i, 0)),
                  pl.BlockSpec((tm, 1), lambda i: (i, 0))] + [full(c) for c in consts],
        out_specs=[pl.BlockSpec((tm, A_H * A_DPAD), lambda i: (i, 0)),
                   pl.BlockSpec((tm, A_H * A_DPAD), lambda i: (i, 0)),
                   pl.BlockSpec((tm, A_H * A_DV), lambda i: (i, 0))],
        out_shape=[jax.ShapeDtypeStruct((T, A_H * A_DPAD), BF16),
                   jax.ShapeDtypeStruct((T, A_H * A_DPAD), BF16),
                   jax.ShapeDtypeStruct((T, A_H * A_DV), BF16)],
        compiler_params=_cparams(("parallel",)),
        name="mla_prep",
    )(lat, pos_col, *consts)


def _attn_kernel(q_ref, k_ref, v_ref, o_ref, *, tq):
    qi = pl.program_id(2)
    q = q_ref[...]

    def chunk(j, carry, masked):
        m, l, acc = carry
        k = k_ref[pl.ds(pl.multiple_of(j * tq, tq), tq), :]
        v = v_ref[pl.ds(pl.multiple_of(j * tq, tq), tq), :]
        s = _dot_nt(q, k)
        if masked:
            row = lax.broadcasted_iota(jnp.int32, (tq, tq), 0)
            col = lax.broadcasted_iota(jnp.int32, (tq, tq), 1)
            s = jnp.where(col <= row, s, NEG_INF)
        m_new = jnp.maximum(m, jnp.max(s, axis=-1, keepdims=True))
        alpha = jnp.exp(m - m_new)
        p = jnp.exp(s - m_new)
        l = alpha * l + jnp.sum(p, axis=-1, keepdims=True)
        acc = alpha * acc + _dot(p.astype(BF16), v)
        return m_new, l, acc

    init = (jnp.full((tq, 1), NEG_INF, F32), jnp.zeros((tq, 1), F32), jnp.zeros((tq, A_DV), F32))
    carry = lax.fori_loop(0, qi, lambda j, c: chunk(j, c, False), init)
    m, l, acc = chunk(qi, carry, True)
    o_ref[...] = (acc / l).astype(BF16)


def _mla_attn(q, k, v, B, S, tq):
    T = B * S
    nq = S // tq
    return pl.pallas_call(
        functools.partial(_attn_kernel, tq=tq),
        grid=(B, A_H, nq),
        in_specs=[pl.BlockSpec((tq, A_DPAD), lambda b, h, i: (b * nq + i, h)),
                  pl.BlockSpec((S, A_DPAD), lambda b, h, i: (b, h)),
                  pl.BlockSpec((S, A_DV), lambda b, h, i: (b, h))],
        out_specs=pl.BlockSpec((tq, A_DV), lambda b, h, i: (b * nq + i, h)),
        out_shape=jax.ShapeDtypeStruct((T, A_H * A_DV), BF16),
        compiler_params=_cparams(("parallel", "parallel", "arbitrary")),
        name="mla_attn",
    )(q, k, v)


def _outproj_kernel(x_ref, ym_ref, ya_ref, wom_ref, woa_ref, g_ref, wq_ref, k1_ref, k2_ref,
                    x1_ref, h2t_ref, s1_ref, s2_ref):
    x1 = x_ref[...] + _dot(ym_ref[...], wom_ref[...]) + _dot(ya_ref[...], woa_ref[...])
    x1_ref[...] = x1
    ms = jnp.mean(x1 * x1, axis=-1, keepdims=True)
    h2f = x1 * lax.rsqrt(ms + RMS_EPS) * g_ref[...]
    h2 = h2f.astype(BF16)
    h2t_ref[...] = h2f.T.astype(BF16)
    qry = _dot(h2, wq_ref[...]).astype(BF16)
    half = P_DK // 2
    for h in range(P_H):
        s1_ref[h] = _dot_nt(k1_ref[...], qry[:, h * P_DK:h * P_DK + half])
        s2_ref[h] = _dot_nt(k2_ref[...], qry[:, h * P_DK + half:(h + 1) * P_DK])


def _out_proj(x2, ym, ya, wo_m, wo_a, g, wq, k1, k2, tm):
    T = x2.shape[0]
    full = lambda a: pl.BlockSpec(a.shape, lambda i: (0, 0))
    return pl.pallas_call(
        _outproj_kernel,
        grid=(T // tm,),
        in_specs=[pl.BlockSpec((tm, D_MODEL), lambda i: (i, 0)),
                  pl.BlockSpec((tm, ym.shape[1]), lambda i: (i, 0)),
                  pl.BlockSpec((tm, ya.shape[1]), lambda i: (i, 0)),
                  full(wo_m), full(wo_a), full(g), full(wq), full(k1), full(k2)],
        out_specs=[pl.BlockSpec((tm, D_MODEL), lambda i: (i, 0)),
                   pl.BlockSpec((D_MODEL, tm), lambda i: (0, i)),
                   pl.BlockSpec((P_H, P_NK, tm), lambda i: (0, 0, i)),
                   pl.BlockSpec((P_H, P_NK, tm), lambda i: (0, 0, i))],
        out_shape=[jax.ShapeDtypeStruct((T, D_MODEL), F32),
                   jax.ShapeDtypeStruct((D_MODEL, T), BF16),
                   jax.ShapeDtypeStruct((P_H, P_NK, T), F32),
                   jax.ShapeDtypeStruct((P_H, P_NK, T), F32)],
        compiler_params=_cparams(("parallel",)),
        name="out_peerq",
    )(x2, ym, ya, wo_m, wo_a, g, wq, k1, k2)


def _top16(x):
    vals = []
    for _ in range(P_TOPK):
        m = jnp.max(x, axis=0, keepdims=True)
        vals.append(m)
        x = jnp.where(x == m, NEG_INF, x)
    return vals


_CAND = [(a, b) for a in range(P_TOPK) for b in range(P_TOPK) if (a + 1) * (b + 1) <= P_TOPK]


_NO_RANK = 64.0


def _route_kernel(s1_ref, s2_ref, r2_ref, e2_ref, cnt_ref, f_ref):
    s1 = s1_ref[0]
    s2 = s2_ref[0]
    v1 = _top16(s1)
    v2 = []
    r2 = jnp.full_like(s2, _NO_RANK)
    x = s2
    for b in range(P_TOPK):
        m = jnp.max(x, axis=0, keepdims=True)
        v2.append(m)
        eq = x == m
        r2 = jnp.where(eq, float(b), r2)
        x = jnp.where(eq, NEG_INF, x)
    cand = jnp.concatenate([v1[a] + v2[b] for a, b in _CAND], axis=0)
    tau = _top16(cand)[P_TOPK - 1]
    top = v1[0] + v2[0]
    z = jnp.zeros_like(tau)
    cnt_a = [jnp.zeros_like(tau) for _ in range(P_TOPK)]
    for a, b in _CAND:
        sm = v1[a] + v2[b]
        sel = sm >= tau
        z = z + jnp.where(sel, jnp.exp(sm - top), 0.0)
        cnt_a[a] = cnt_a[a] + jnp.where(sel, 1.0, 0.0)
    cnt = jnp.zeros_like(s1)
    for a in range(P_TOPK):
        cnt = jnp.where(s1 == v1[a], cnt_a[a], cnt)
    r2_ref[0] = r2
    e2_ref[0] = jnp.exp(s2 - v2[0])
    cnt_ref[0] = cnt
    f_ref[0] = jnp.exp(s1 - v1[0]) / z


def _peer_route(s1t, s2t, tl):
    T = s1t.shape[2]
    spec = pl.BlockSpec((1, P_NK, tl), lambda h, i: (h, 0, i))
    shp32 = jax.ShapeDtypeStruct((P_H, P_NK, T), F32)
    return pl.pallas_call(
        _route_kernel,
        grid=(P_H, T // tl),
        in_specs=[spec, spec],
        out_specs=[spec, spec, spec, spec],
        out_shape=[shp32, shp32, shp32, shp32],
        compiler_params=_cparams(("parallel", "parallel")),
        name="peer_route",
    )(s1t, s2t)


def _gelu(x):
    return 0.5 * x * (1.0 + lax.erf(x * (2.0 ** -0.5)))


_LANES = 128
_PACK = 16
_N1_GROUP = 4


def _peer_kernel(h2t_ref, x1_ref, u_ref, vt_ref, r2f_ref, e2f_ref, cnt_ref, f_ref, o_ref,
                 acc_ref, a_ref, p_ref, r2_ref, e2_ref, *, nb, tm):
    e = pl.program_id(1)

    @pl.when(e == 0)
    def _():
        acc_ref[...] = jnp.zeros_like(acc_ref)
        for h in range(P_H):
            r2_ref[h] = r2f_ref[h].astype(BF16)
            e2_ref[h] = e2f_ref[h].astype(BF16)

    a_ref[...] = _dot(u_ref[...], h2t_ref[...])
    n1_0 = pl.multiple_of(e * nb, nb)
    zero = jnp.zeros((_PACK, _LANES), BF16)
    for g in range(tm // _LANES):
        lanes = slice(g * _LANES, (g + 1) * _LANES)
        for jc in range(nb // _N1_GROUP):
            w = [[None] * (P_NK // _PACK) for _ in range(_N1_GROUP)]
            for h in range(P_H):
                cnt8 = cnt_ref[h, pl.ds(n1_0, nb), lanes]
                f8 = f_ref[h, pl.ds(n1_0, nb), lanes]
                for jj in range(_N1_GROUP):
                    j = jc * _N1_GROUP + jj
                    cb = jnp.broadcast_to(cnt8[j:j + 1, :], (_PACK, _LANES)).astype(BF16)
                    fb = jnp.broadcast_to(f8[j:j + 1, :], (_PACK, _LANES)).astype(BF16)
                    for r in range(P_NK // _PACK):
                        rows = slice(r * _PACK, (r + 1) * _PACK)
                        t = jnp.where(r2_ref[h, rows, lanes] < cb, e2_ref[h, rows, lanes], zero) * fb
                        w[jj][r] = t if w[jj][r] is None else w[jj][r] + t
            for jj in range(_N1_GROUP):
                j = jc * _N1_GROUP + jj
                for r in range(P_NK // _PACK):
                    rows = slice(j * P_NK + r * _PACK, j * P_NK + (r + 1) * _PACK)
                    p_ref[rows, lanes] = w[jj][r] * _gelu(a_ref[rows, lanes]).astype(BF16)
    acc_ref[...] += _dot(vt_ref[...], p_ref[...])

    @pl.when(e == pl.num_programs(1) - 1)
    def _():
        o_ref[...] = x1_ref[...] + acc_ref[...].T


def _peer_mlp(h2t, x1, u, vt, r2t, e2t, cntt, ft, tm, nb):
    T = x1.shape[0]
    ne = P_NK // nb
    te = nb * P_NK
    rspec = pl.BlockSpec((P_H, P_NK, tm), lambda i, e: (0, 0, i))
    return pl.pallas_call(
        functools.partial(_peer_kernel, nb=nb, tm=tm),
        grid=(T // tm, ne),
        in_specs=[pl.BlockSpec((D_MODEL, tm), lambda i, e: (0, i)),
                  pl.BlockSpec((tm, D_MODEL), lambda i, e: (i, 0)),
                  pl.BlockSpec((te, D_MODEL), lambda i, e: (e, 0)),
                  pl.BlockSpec((D_MODEL, te), lambda i, e: (0, e)),
                  rspec, rspec, rspec, rspec],
        out_specs=pl.BlockSpec((tm, D_MODEL), lambda i, e: (i, 0)),
        out_shape=jax.ShapeDtypeStruct((T, D_MODEL), F32),
        scratch_shapes=[pltpu.VMEM((D_MODEL, tm), F32),
                        pltpu.VMEM((te, tm), F32),
                        pltpu.VMEM((te, tm), BF16),
                        pltpu.VMEM((P_H, P_NK, tm), BF16),
                        pltpu.VMEM((P_H, P_NK, tm), BF16)],
        compiler_params=_cparams(("parallel", "arbitrary")),
        name="peer_mlp",
    )(h2t, x1, u, vt, r2t, e2t, cntt, ft)


def _swap_halves(t, axis=-1):
    a, b = jnp.split(t, 2, axis=axis)
    return jnp.concatenate([b, a], axis=axis)


def _pad_cols(t, n):
    return jnp.pad(t, ((0, 0), (0, n - t.shape[1])))


def _layer(x2, pos_col, B, S, attn_norm_g, w_in, b_igate, b_fgate, mlstm_norm_g, cq_norm_g, w_uq,
           ckv_norm_g, w_ukv, q_norm_g, k_norm_g, w_out, ffn_norm_g, peer_w_q, peer_keys_1,
           peer_keys_2, peer_u, peer_v):
    T = B * S
    n_qkvo = 4 * M_H * M_DK
    o_gate = n_qkvo
    o_cq = o_gate + 2 * M_H
    o_ckv = o_cq + A_QR
    o_kr = o_ckv + A_KVR

    w_main = w_in[:, :n_qkvo].astype(BF16)
    w_kr = w_in[:, o_kr:o_kr + A_ROPE]
    w_lat = jnp.concatenate([w_in[:, o_cq:o_kr], _pad_cols(w_kr, 128),
                             _pad_cols(_swap_halves(w_kr), 128)], axis=1).astype(BF16)
    w_gate = _pad_cols(w_in[:, o_gate:o_gate + 2 * M_H], 128).astype(BF16)
    bias_row = _pad_cols(jnp.concatenate([b_igate, b_fgate])[None, :], 128)

    wq_h = w_uq.reshape(A_QR, A_H, A_DQK)
    wq_pad = jnp.pad(wq_h, ((0, 0), (0, 0), (0, A_DPAD - A_DQK))).reshape(A_QR, A_H * A_DPAD).astype(BF16)
    wq_sw = jnp.pad(_swap_halves(wq_h[:, :, A_NOPE:]), ((0, 0), (0, 0), (0, 128 - A_ROPE)))
    wq_sw = wq_sw.reshape(A_QR, A_H * 128).astype(BF16)
    w_kv = w_ukv.astype(BF16)
    qg = _pad_cols(q_norm_g[None, :], A_DPAD)
    qgs = _pad_cols(_swap_halves(q_norm_g[A_NOPE:])[None, :], 128)
    kg = _pad_cols(k_norm_g[None, :], A_DPAD)
    kgs = _pad_cols(_swap_halves(k_norm_g[A_NOPE:])[None, :], 128)
    inv_freq = ROPE_THETA ** (-jnp.arange(0, A_ROPE, 2, dtype=F32) / A_ROPE)
    freq_row = _pad_cols(jnp.concatenate([inv_freq, inv_freq])[None, :], 128)
    half = A_ROPE // 2
    sign_row = _pad_cols(jnp.concatenate([-jnp.ones((half,), F32), jnp.ones((half,), F32)])[None, :], 128)

    qkvo, lat, gates = _in_proj(x2, attn_norm_g[None, :], w_main, w_lat, w_gate, tm=min(512, T))
    y_m = _mlstm(qkvo, gates, bias_row, mlstm_norm_g[None, :], B, S, L=min(256, S))
    consts = [cq_norm_g[None, :], ckv_norm_g[None, :], wq_pad, wq_sw, w_kv, qg, qgs, kg, kgs,
              freq_row, sign_row]
    q, k, v = _mla_prep(lat, pos_col, consts, tm=min(512, T))
    y_a = _mla_attn(q, k, v, B, S, tq=min(512, S))
    wo = w_out.astype(BF16)
    x1, h2t, s1t, s2t = _out_proj(x2, y_m, y_a, wo[:M_H * M_DV], wo[M_H * M_DV:], ffn_norm_g[None, :],
                                  peer_w_q.astype(BF16), peer_keys_1.astype(BF16),
                                  peer_keys_2.astype(BF16), tm=min(512, T))
    r2t, e2t, cntt, ft = _peer_route(s1t, s2t, tl=min(256, T))
    return _peer_mlp(h2t, x1, peer_u.astype(BF16), peer_v.astype(BF16).T, r2t, e2t, cntt, ft,
                     tm=min(512, T), nb=8)


def kernel(x, positions, attn_norm_g, w_in, b_igate, b_fgate, mlstm_norm_g, cq_norm_g, w_uq, ckv_norm_g,
           w_ukv, q_norm_g, k_norm_g, w_out, ffn_norm_g, peer_w_q, peer_keys_1, peer_keys_2, peer_u, peer_v):
    B, S, D = x.shape
    x2 = x.reshape(B * S, D)
    pos_col = positions.reshape(B * S, 1).astype(F32)
    for l in range(attn_norm_g.shape[0]):
        x2 = _layer(x2, pos_col, B, S, attn_norm_g[l], w_in[l], b_igate[l], b_fgate[l], mlstm_norm_g[l],
                    cq_norm_g[l], w_uq[l], ckv_norm_g[l], w_ukv[l], q_norm_g[l], k_norm_g[l], w_out[l],
                    ffn_norm_g[l], peer_w_q[l], peer_keys_1[l], peer_keys_2[l], peer_u[l], peer_v[l])
    return x2.reshape(B, S, D)
```

```python
import functools

import jax
import jax.numpy as jnp
from jax import lax
from jax.experimental import pallas as pl
from jax.experimental.pallas import tpu as pltpu

F32 = jnp.float32
BF16 = jnp.bfloat16

D_MODEL = 1024
RMS_EPS = 1e-6
M_H, M_DK, M_DV = 4, 128, 128
A_H, A_QR, A_KVR, A_NOPE, A_ROPE, A_DV = 4, 256, 128, 128, 64, 128
A_DQK = A_NOPE + A_ROPE
A_DPAD = 256
ROPE_THETA = 10000.0
P_H, P_NK, P_DK, P_TOPK = 8, 128, 256, 16
NEG_INF = float("-inf")

VMEM_LIMIT = 56 * 1024 * 1024


def _cparams(sem):
    return pltpu.CompilerParams(dimension_semantics=sem, vmem_limit_bytes=VMEM_LIMIT)


def _dot(a, b):
    return jnp.dot(a, b, preferred_element_type=F32)


def _dot_nt(a, b):
    return lax.dot_general(a, b, (((1,), (1,)), ((), ())), preferred_element_type=F32)


def _dot_tn(a, b):
    return lax.dot_general(a, b, (((0,), (0,)), ((), ())), preferred_element_type=F32)


def _inproj_kernel(x_ref, g_ref, wm_ref, wl_ref, wg_ref, qkvo_ref, lat_ref, gate_ref):
    x = x_ref[...]
    ms = jnp.mean(x * x, axis=-1, keepdims=True)
    h = (x * lax.rsqrt(ms + RMS_EPS) * g_ref[...]).astype(BF16)
    main = _dot(h, wm_ref[...])
    kcol = lax.broadcasted_iota(jnp.int32, (1, main.shape[1]), 1) // (M_H * M_DK) == 1
    main = jnp.where(kcol, main * (M_DK ** -0.5), main)
    qkvo_ref[...] = main.astype(BF16)
    lat_ref[...] = _dot(h, wl_ref[...])
    gate_ref[...] = _dot(h, wg_ref[...])


def _in_proj(x2, g, w_main, w_lat, w_gate, tm):
    T = x2.shape[0]
    n_main, n_lat, n_gate = w_main.shape[1], w_lat.shape[1], w_gate.shape[1]
    full = lambda shape: pl.BlockSpec(shape, lambda i: (0, 0))
    return pl.pallas_call(
        _inproj_kernel,
        grid=(T // tm,),
        in_specs=[pl.BlockSpec((tm, D_MODEL), lambda i: (i, 0)), full((1, D_MODEL)),
                  full(w_main.shape), full(w_lat.shape), full(w_gate.shape)],
        out_specs=[pl.BlockSpec((tm, n_main), lambda i: (i, 0)),
                   pl.BlockSpec((tm, n_lat), lambda i: (i, 0)),
                   pl.BlockSpec((tm, n_gate), lambda i: (i, 0))],
        out_shape=[jax.ShapeDtypeStruct((T, n_main), BF16),
                   jax.ShapeDtypeStruct((T, n_lat), F32),
                   jax.ShapeDtypeStruct((T, n_gate), F32)],
        compiler_params=_cparams(("parallel",)),
        name="in_proj",
    )(x2, g, w_main, w_lat, w_gate)


def _log_sigmoid(x):
    return -(jnp.maximum(-x, 0.0) + jnp.log1p(jnp.exp(-jnp.abs(x))))


def _mlstm_kernel(qkvo_ref, gate_ref, bias_ref, ng_ref, y_ref, c_ref, m_ref, *, L):
    c_idx = pl.program_id(1)

    @pl.when(c_idx == 0)
    def _():
        c_ref[...] = jnp.zeros_like(c_ref)
        m_ref[...] = jnp.zeros_like(m_ref)

    H, DK, DV = M_H, M_DK, M_DV
    G = gate_ref[...] + bias_ref[...]
    LF = _log_sigmoid(G)
    row = lax.broadcasted_iota(jnp.int32, (L, L), 0)
    col = lax.broadcasted_iota(jnp.int32, (L, L), 1)
    causal = col <= row
    tri = jnp.where(causal, 1.0, 0.0)
    Bc = jnp.dot(tri, LF, preferred_element_type=F32, precision=lax.Precision.HIGHEST)
    lane = lax.broadcasted_iota(jnp.int32, (L, 128), 1)
    ZT = jnp.where(lane < H, G, Bc).T
    ones_col = jnp.where(lax.broadcasted_iota(jnp.int32, (L, 128), 1) == 0, 1.0, 0.0).astype(BF16)

    for h in range(H):
        q = qkvo_ref[:, h * DK:(h + 1) * DK]
        k = qkvo_ref[:, H * DK + h * DK: H * DK + (h + 1) * DK]
        v = qkvo_ref[:, 2 * H * DK + h * DV: 2 * H * DK + (h + 1) * DV]
        o = qkvo_ref[:, 3 * H * DK + h * DV: 3 * H * DK + (h + 1) * DV]
        i_col = G[:, h:h + 1]
        b_col = Bc[:, H + h:H + h + 1]
        i_row = ZT[h:h + 1, :]
        b_row = ZT[H + h:H + h + 1, :]
        m_prev = m_ref[h]
        c_prev = c_ref[h]

        a_col = b_col + m_prev
        D = jnp.where(causal, b_col - b_row + i_row, NEG_INF)
        m_t = jnp.maximum(a_col, jnp.max(D, axis=-1, keepdims=True))
        Dw = jnp.exp(D - m_t)
        aw = jnp.exp(a_col - m_t)
        s = (_dot_nt(q, k) * Dw).astype(BF16)
        v_aug = jnp.concatenate([v, ones_col], axis=-1)
        num = aw * _dot(q, c_prev.astype(BF16)) + _dot(s, v_aug)
        den = num[:, DV:DV + 1]
        hval = num[:, :DV] / jnp.maximum(jnp.abs(den), jnp.exp(-m_t))

        bL = b_col[L - 1:L, :]
        g_col = bL - b_col + i_col
        m_new = jnp.maximum(bL + m_prev, jnp.max(g_col, axis=0, keepdims=True))
        decay = jnp.exp(bL + m_prev - m_new)
        w_col = jnp.exp(g_col - m_new)
        wv = (w_col * v_aug.astype(F32)).astype(BF16)
        c_ref[h] = decay * c_prev + _dot_tn(k, wv)
        m_ref[h] = m_new

        ms = jnp.mean(hval * hval, axis=-1, keepdims=True)
        hn = hval * lax.rsqrt(ms + RMS_EPS) * ng_ref[:, h * DV:(h + 1) * DV]
        y_ref[:, h * DV:(h + 1) * DV] = (jax.nn.sigmoid(o.astype(F32)) * hn).astype(BF16)


def _mlstm(qkvo, gates, bias_row, norm_g, B, S, L, name="mlstm"):
    T = B * S
    nc = S // L
    return pl.pallas_call(
        functools.partial(_mlstm_kernel, L=L),
        grid=(B, nc),
        in_specs=[pl.BlockSpec((L, qkvo.shape[1]), lambda b, c: (b * nc + c, 0)),
                  pl.BlockSpec((L, 128), lambda b, c: (b * nc + c, 0)),
                  pl.BlockSpec((1, 128), lambda b, c: (0, 0)),
                  pl.BlockSpec((1, M_H * M_DV), lambda b, c: (0, 0))],
        out_specs=pl.BlockSpec((L, M_H * M_DV), lambda b, c: (b * nc + c, 0)),
        out_shape=jax.ShapeDtypeStruct((T, M_H * M_DV), BF16),
        scratch_shapes=[pltpu.VMEM((M_H, M_DK, 2 * M_DV), F32),
                        pltpu.VMEM((M_H, 1, 1), F32)],
        compiler_params=_cparams(("parallel", "arbitrary")),
        name=name,
    )(qkvo, gates, bias_row, norm_g)


def _mla_prep_kernel(lat_ref, pos_ref, cqg_ref, ckvg_ref, wq_ref, wqs_ref, wkv_ref,
                     qg_ref, qgs_ref, kg_ref, kgs_ref, freq_ref, sign_ref,
                     q_ref, k_ref, v_ref):
    H = A_H
    cq = lat_ref[:, :A_QR]
    ckv = lat_ref[:, A_QR:A_QR + A_KVR]
    kr = lat_ref[:, A_QR + A_KVR:A_QR + A_KVR + 128]
    krs = lat_ref[:, A_QR + A_KVR + 128:A_QR + A_KVR + 256]

    def rms(t, g):
        return t * lax.rsqrt(jnp.mean(t * t, axis=-1, keepdims=True) + RMS_EPS) * g

    cqn = rms(cq, cqg_ref[...]).astype(BF16)
    ckvn = rms(ckv, ckvg_ref[...]).astype(BF16)
    qf = _dot(cqn, wq_ref[...])
    qs = _dot(cqn, wqs_ref[...])
    kv = _dot(ckvn, wkv_ref[...])

    ang = pos_ref[...] * freq_ref[...]
    cos = jnp.cos(ang)
    sin = jnp.sin(ang) * sign_ref[...]
    scale = A_DQK ** -0.5
    kr_ss = jnp.sum(kr * kr, axis=-1, keepdims=True)

    for h in range(H):
        qh = qf[:, h * A_DPAD:(h + 1) * A_DPAD]
        rstd = lax.rsqrt(jnp.sum(qh * qh, axis=-1, keepdims=True) / A_DQK + RMS_EPS) * scale
        q_nope = qh[:, :128] * rstd * qg_ref[:, :128]
        q_rope = qh[:, 128:] * rstd * qg_ref[:, 128:]
        q_rope_s = qs[:, h * 128:(h + 1) * 128] * rstd * qgs_ref[...]
        q_ref[:, h * A_DPAD:h * A_DPAD + 128] = q_nope.astype(BF16)
        q_ref[:, h * A_DPAD + 128:(h + 1) * A_DPAD] = (q_rope * cos + q_rope_s * sin).astype(BF16)

        kn = kv[:, h * 256:h * 256 + 128]
        rstd_k = lax.rsqrt((jnp.sum(kn * kn, axis=-1, keepdims=True) + kr_ss) / A_DQK + RMS_EPS)
        k_nope = kn * rstd_k * kg_ref[:, :128]
        k_rope = kr * rstd_k * kg_ref[:, 128:]
        k_rope_s = krs * rstd_k * kgs_ref[...]
        k_ref[:, h * A_DPAD:h * A_DPAD + 128] = k_nope.astype(BF16)
        k_ref[:, h * A_DPAD + 128:(h + 1) * A_DPAD] = (k_rope * cos + k_rope_s * sin).astype(BF16)
        v_ref[:, h * A_DV:(h + 1) * A_DV] = kv[:, h * 256 + 128:(h + 1) * 256].astype(BF16)


def _mla_prep(lat, pos_col, consts, tm):
    T = lat.shape[0]
    full = lambda a: pl.BlockSpec(a.shape, lambda i: (0, 0))
    return pl.pallas_call(
        _mla_prep_kernel,
        grid=(T // tm,),
        in_specs=[pl.BlockSpec((tm, lat.shape[1]), lambda i: (i, 0)),
                  pl.BlockSpec((tm, 1), lambda i: (i, 0))] + [full(c) for c in consts],
        out_specs=[pl.BlockSpec((tm, A_H * A_DPAD), lambda i: (i, 0)),
                   pl.BlockSpec((tm, A_H * A_DPAD), lambda i: (i, 0)),
                   pl.BlockSpec((tm, A_H * A_DV), lambda i: (i, 0))],
        out_shape=[jax.ShapeDtypeStruct((T, A_H * A_DPAD), BF16),
                   jax.ShapeDtypeStruct((T, A_H * A_DPAD), BF16),
                   jax.ShapeDtypeStruct((T, A_H * A_DV), BF16)],
        compiler_params=_cparams(("parallel",)),
        name="mla_prep",
    )(lat, pos_col, *consts)


def _attn_kernel(q_ref, k_ref, v_ref, o_ref, *, tq):
    qi = pl.program_id(2)
    q = q_ref[...]

    def chunk(j, carry, masked):
        m, l, acc = carry
        k = k_ref[pl.ds(pl.multiple_of(j * tq, tq), tq), :]
        v = v_ref[pl.ds(pl.multiple_of(j * tq, tq), tq), :]
        s = _dot_nt(q, k)
        if masked:
            row = lax.broadcasted_iota(jnp.int32, (tq, tq), 0)
            col = lax.broadcasted_iota(jnp.int32, (tq, tq), 1)
            s = jnp.where(col <= row, s, NEG_INF)
        m_new = jnp.maximum(m, jnp.max(s, axis=-1, keepdims=True))
        alpha = jnp.exp(m - m_new)
        p = jnp.exp(s - m_new)
        l = alpha * l + jnp.sum(p, axis=-1, keepdims=True)
        acc = alpha * acc + _dot(p.astype(BF16), v)
        return m_new, l, acc

    init = (jnp.full((tq, 1), NEG_INF, F32), jnp.zeros((tq, 1), F32), jnp.zeros((tq, A_DV), F32))
    carry = lax.fori_loop(0, qi, lambda j, c: chunk(j, c, False), init)
    m, l, acc = chunk(qi, carry, True)
    o_ref[...] = (acc / l).astype(BF16)


def _mla_attn(q, k, v, B, S, tq, name="mla_attn"):
    T = B * S
    nq = S // tq
    return pl.pallas_call(
        functools.partial(_attn_kernel, tq=tq),
        grid=(B, A_H, nq),
        in_specs=[pl.BlockSpec((tq, A_DPAD), lambda b, h, i: (b * nq + i, h)),
                  pl.BlockSpec((S, A_DPAD), lambda b, h, i: (b, h)),
                  pl.BlockSpec((S, A_DV), lambda b, h, i: (b, h))],
        out_specs=pl.BlockSpec((tq, A_DV), lambda b, h, i: (b * nq + i, h)),
        out_shape=jax.ShapeDtypeStruct((T, A_H * A_DV), BF16),
        compiler_params=_cparams(("parallel", "parallel", "arbitrary")),
        name=name,
    )(q, k, v)


def _outproj_kernel(x_ref, ym_ref, ya_ref, wom_ref, woa_ref, g_ref, wq_ref, k1_ref, k2_ref,
                    x1_ref, h2t_ref, s1_ref, s2_ref):
    x1 = x_ref[...] + _dot(ym_ref[...], wom_ref[...]) + _dot(ya_ref[...], woa_ref[...])
    x1_ref[...] = x1
    ms = jnp.mean(x1 * x1, axis=-1, keepdims=True)
    h2f = x1 * lax.rsqrt(ms + RMS_EPS) * g_ref[...]
    h2 = h2f.astype(BF16)
    h2t_ref[...] = pltpu.bitcast(h2f.T.astype(BF16), jnp.uint32)
    qry = _dot(h2, wq_ref[...]).astype(BF16)
    half = P_DK // 2
    for h in range(P_H):
        s1_ref[h] = _dot_nt(k1_ref[...], qry[:, h * P_DK:h * P_DK + half])
        s2_ref[h] = _dot_nt(k2_ref[...], qry[:, h * P_DK + half:(h + 1) * P_DK])


def _out_proj(x2, ym, ya, wo_m, wo_a, g, wq, k1, k2, tm):
    T = x2.shape[0]
    full = lambda a: pl.BlockSpec(a.shape, lambda i: (0, 0))
    return pl.pallas_call(
        _outproj_kernel,
        grid=(T // tm,),
        in_specs=[pl.BlockSpec((tm, D_MODEL), lambda i: (i, 0)),
                  pl.BlockSpec((tm, ym.shape[1]), lambda i: (i, 0)),
                  pl.BlockSpec((tm, ya.shape[1]), lambda i: (i, 0)),
                  full(wo_m), full(wo_a), full(g), full(wq), full(k1), full(k2)],
        out_specs=[pl.BlockSpec((tm, D_MODEL), lambda i: (i, 0)),
                   pl.BlockSpec((D_MODEL // 2, tm), lambda i: (0, i)),
                   pl.BlockSpec((P_H, P_NK, tm), lambda i: (0, 0, i)),
                   pl.BlockSpec((P_H, P_NK, tm), lambda i: (0, 0, i))],
        out_shape=[jax.ShapeDtypeStruct((T, D_MODEL), F32),
                   jax.ShapeDtypeStruct((D_MODEL // 2, T), jnp.uint32),
                   jax.ShapeDtypeStruct((P_H, P_NK, T), F32),
                   jax.ShapeDtypeStruct((P_H, P_NK, T), F32)],
        compiler_params=_cparams(("parallel",)),
        name="out_peerq",
    )(x2, ym, ya, wo_m, wo_a, g, wq, k1, k2)


def _top16(x):
    vals = []
    for _ in range(P_TOPK):
        m = jnp.max(x, axis=0, keepdims=True)
        vals.append(m)
        x = jnp.where(x == m, NEG_INF, x)
    return vals


_CAND = [(a, b) for a in range(P_TOPK) for b in range(P_TOPK) if (a + 1) * (b + 1) <= P_TOPK]


_NO_RANK = 64.0


def _route_kernel(s1_ref, s2_ref, r2_ref, e2_ref, cnt_ref, f_ref):
    s1 = s1_ref[0]
    s2 = s2_ref[0]
    v1 = _top16(s1)
    v2 = []
    r2 = jnp.full_like(s2, _NO_RANK)
    x = s2
    for b in range(P_TOPK):
        m = jnp.max(x, axis=0, keepdims=True)
        v2.append(m)
        eq = x == m
        r2 = jnp.where(eq, float(b), r2)
        x = jnp.where(eq, NEG_INF, x)
    cand = jnp.concatenate([v1[a] + v2[b] for a, b in _CAND], axis=0)
    tau = _top16(cand)[P_TOPK - 1]
    top = v1[0] + v2[0]
    z = jnp.zeros_like(tau)
    cnt_a = [jnp.zeros_like(tau) for _ in range(P_TOPK)]
    for a, b in _CAND:
        sm = v1[a] + v2[b]
        sel = sm >= tau
        z = z + jnp.where(sel, jnp.exp(sm - top), 0.0)
        cnt_a[a] = cnt_a[a] + jnp.where(sel, 1.0, 0.0)
    cnt = jnp.zeros_like(s1)
    for a in range(P_TOPK):
        cnt = jnp.where(s1 == v1[a], cnt_a[a], cnt)
    r2_ref[0] = r2
    e2_ref[0] = jnp.exp(s2 - v2[0])
    cnt_ref[0] = cnt
    f_ref[0] = jnp.exp(s1 - v1[0]) / z


def _peer_route(s1t, s2t, tl):
    T = s1t.shape[2]
    spec = pl.BlockSpec((1, P_NK, tl), lambda h, i: (h, 0, i))
    shp32 = jax.ShapeDtypeStruct((P_H, P_NK, T), F32)
    return pl.pallas_call(
        _route_kernel,
        grid=(P_H, T // tl),
        in_specs=[spec, spec],
        out_specs=[spec, spec, spec, spec],
        out_shape=[shp32, shp32, shp32, shp32],
        compiler_params=_cparams(("parallel", "parallel")),
        name="peer_route",
    )(s1t, s2t)


def _gelu(x):
    return 0.5 * x * (1.0 + lax.erf(x * (2.0 ** -0.5)))


def _gelu2(x):
    return x * (1.0 + lax.erf(x * (2.0 ** -0.5)))


_LANES = 128
_PACK = 16
_N1_GROUP = 4


def _peer_kernel(h2t_ref, x1_ref, u_ref, vt_ref, r2f_ref, e2f_ref, cnt_ref, f_ref, o_ref,
                 acc_ref, a_ref, p_ref, r2_ref, e2_ref, *, nb, tm, fold_half):
    e = pl.program_id(1)

    @pl.when(e == 0)
    def _():
        acc_ref[...] = jnp.zeros_like(acc_ref)
        for h in range(P_H):
            r2_ref[h] = r2f_ref[h].astype(BF16)
            e2_ref[h] = e2f_ref[h].astype(BF16)

    a_ref[...] = _dot(pltpu.bitcast(u_ref[...], BF16), pltpu.bitcast(h2t_ref[...], BF16))
    n1_0 = pl.multiple_of(e * nb, nb)
    zero = jnp.zeros((_PACK, _LANES), BF16)
    for g in range(tm // _LANES):
        lanes = slice(g * _LANES, (g + 1) * _LANES)
        for jc in range(nb // _N1_GROUP):
            w = [[None] * (P_NK // _PACK) for _ in range(_N1_GROUP)]
            for h in range(P_H):
                cnt8 = cnt_ref[h, pl.ds(n1_0, nb), lanes]
                f8 = f_ref[h, pl.ds(n1_0, nb), lanes]
                if fold_half:
                    f8 = 0.5 * f8
                for jj in range(_N1_GROUP):
                    j = jc * _N1_GROUP + jj
                    cb = jnp.broadcast_to(cnt8[j:j + 1, :], (_PACK, _LANES)).astype(BF16)
                    fb = jnp.broadcast_to(f8[j:j + 1, :], (_PACK, _LANES)).astype(BF16)
                    for r in range(P_NK // _PACK):
                        rows = slice(r * _PACK, (r + 1) * _PACK)
                        t = jnp.where(r2_ref[h, rows, lanes] < cb, e2_ref[h, rows, lanes], zero) * fb
                        w[jj][r] = t if w[jj][r] is None else w[jj][r] + t
            for jj in range(_N1_GROUP):
                j = jc * _N1_GROUP + jj
                for r in range(P_NK // _PACK):
                    rows = slice(j * P_NK + r * _PACK, j * P_NK + (r + 1) * _PACK)
                    act = _gelu2(a_ref[rows, lanes]) if fold_half else _gelu(a_ref[rows, lanes])
                    p_ref[rows, lanes] = w[jj][r] * act.astype(BF16)
    acc_ref[...] += _dot(pltpu.bitcast(vt_ref[...], BF16), p_ref[...])

    @pl.when(e == pl.num_programs(1) - 1)
    def _():
        o_ref[...] = x1_ref[...] + acc_ref[...].T


def _peer_mlp(h2t, x1, u, vt, r2t, e2t, cntt, ft, tm, nb, fold_half=False, name="peer_mlp"):
    T = x1.shape[0]
    ne = P_NK // nb
    te = nb * P_NK
    rspec = pl.BlockSpec((P_H, P_NK, tm), lambda i, e: (0, 0, i))
    return pl.pallas_call(
        functools.partial(_peer_kernel, nb=nb, tm=tm, fold_half=fold_half),
        grid=(T // tm, ne),
        in_specs=[pl.BlockSpec((D_MODEL // 2, tm), lambda i, e: (0, i)),
                  pl.BlockSpec((tm, D_MODEL), lambda i, e: (i, 0)),
                  pl.BlockSpec((te // 2, D_MODEL), lambda i, e: (e, 0)),
                  pl.BlockSpec((D_MODEL // 2, te), lambda i, e: (0, e)),
                  rspec, rspec, rspec, rspec],
        out_specs=pl.BlockSpec((tm, D_MODEL), lambda i, e: (i, 0)),
        out_shape=jax.ShapeDtypeStruct((T, D_MODEL), F32),
        scratch_shapes=[pltpu.VMEM((D_MODEL, tm), F32),
                        pltpu.VMEM((te, tm), F32),
                        pltpu.VMEM((te, tm), BF16),
                        pltpu.VMEM((P_H, P_NK, tm), BF16),
                        pltpu.VMEM((P_H, P_NK, tm), BF16)],
        compiler_params=_cparams(("parallel", "arbitrary")),
        name=name,
    )(h2t, x1, u, vt, r2t, e2t, cntt, ft)


def _pack_rows(t):
    r, c = t.shape
    return lax.bitcast_convert_type(t.reshape(r // 2, 2, c).swapaxes(1, 2), jnp.uint32)


_CHUNKS = 4


def _peer_kernel_p(h2t_ref, x1_ref, u_ref, vt_ref, r2f_ref, e2f_ref, cnt_ref, f_ref, o_ref,
                   acc_ref, a0_ref, a1_ref, p0_ref, p1_ref, r2_ref, e2_ref, *, nb, tm):
    s = pl.program_id(1)
    ne = pl.num_programs(1) - 2

    @pl.when(s == 0)
    def _():
        acc_ref[...] = jnp.zeros_like(acc_ref)
        a1_ref[...] = jnp.zeros_like(a1_ref)
        p0_ref[...] = jnp.zeros_like(p0_ref)
        for h in range(P_H):
            r2_ref[h] = r2f_ref[h].astype(BF16)
            e2_ref[h] = e2f_ref[h].astype(BF16)

    n1_0 = jnp.clip(s - 1, 0, ne - 1) * nb
    te = nb * P_NK
    ce = te // _CHUNKS
    cd = D_MODEL // _CHUNKS
    nj = ce // P_NK

    def step(a_cur, a_prv, p_cur, p_prv):
        def chunk(k, carry):
            rows_e = pl.ds(pl.multiple_of(k * ce, ce), ce)
            rows_u = pl.ds(pl.multiple_of(k * (ce // 2), ce // 2), ce // 2)
            a_cur[rows_e, :] = _dot(pltpu.bitcast(u_ref[rows_u, :], BF16), pltpu.bitcast(h2t_ref[...], BF16))
            rows_d = pl.ds(pl.multiple_of(k * cd, cd), cd)
            rows_v = pl.ds(pl.multiple_of(k * (cd // 2), cd // 2), cd // 2)
            acc_ref[rows_d, :] += _dot(pltpu.bitcast(vt_ref[rows_v, :], BF16), p_cur[...])
            zero = jnp.zeros((_PACK, _LANES), BF16)
            for g in range(tm // _LANES):
                lanes = slice(g * _LANES, (g + 1) * _LANES)
                for jj in range(nj):
                    n1 = n1_0 + k * nj + jj
                    w = [None] * (P_NK // _PACK)
                    for h in range(P_H):
                        cb = jnp.broadcast_to(cnt_ref[h, pl.ds(n1, 1), :][:, lanes], (_PACK, _LANES)).astype(BF16)
                        fb = jnp.broadcast_to(f_ref[h, pl.ds(n1, 1), :][:, lanes], (_PACK, _LANES)).astype(BF16)
                        for r in range(P_NK // _PACK):
                            rows = slice(r * _PACK, (r + 1) * _PACK)
                            t = jnp.where(r2_ref[h, rows, lanes] < cb, e2_ref[h, rows, lanes], zero) * fb
                            w[r] = t if w[r] is None else w[r] + t
                    for r in range(P_NK // _PACK):
                        rows = pl.ds(pl.multiple_of(k * ce + jj * P_NK + r * _PACK, _PACK), _PACK)
                        p_prv[rows, lanes] = w[r] * _gelu(a_prv[rows, lanes]).astype(BF16)
            return carry

        lax.fori_loop(0, _CHUNKS, chunk, 0)

    @pl.when(s % 2 == 0)
    def _():
        step(a0_ref, a1_ref, p0_ref, p1_ref)

    @pl.when(s % 2 == 1)
    def _():
        step(a1_ref, a0_ref, p1_ref, p0_ref)

    @pl.when(s == pl.num_programs(1) - 1)
    def _():
        o_ref[...] = x1_ref[...] + acc_ref[...].T


def _peer_mlp_p(h2t, x1, u, vt, r2t, e2t, cntt, ft, tm, nb, name="peer_mlp_p"):
    T = x1.shape[0]
    ne = P_NK // nb
    te = nb * P_NK
    rspec = pl.BlockSpec((P_H, P_NK, tm), lambda i, s: (0, 0, i))
    return pl.pallas_call(
        functools.partial(_peer_kernel_p, nb=nb, tm=tm),
        grid=(T // tm, ne + 2),
        in_specs=[pl.BlockSpec((D_MODEL // 2, tm), lambda i, s: (0, i)),
                  pl.BlockSpec((tm, D_MODEL), lambda i, s: (i, 0)),
                  pl.BlockSpec((te // 2, D_MODEL), lambda i, s: (jnp.minimum(s, ne - 1), 0)),
                  pl.BlockSpec((D_MODEL // 2, te), lambda i, s: (0, jnp.clip(s - 2, 0, ne - 1))),
                  rspec, rspec, rspec, rspec],
        out_specs=pl.BlockSpec((tm, D_MODEL), lambda i, s: (i, 0)),
        out_shape=jax.ShapeDtypeStruct((T, D_MODEL), F32),
        scratch_shapes=[pltpu.VMEM((D_MODEL, tm), F32),
                        pltpu.VMEM((te, tm), F32), pltpu.VMEM((te, tm), F32),
                        pltpu.VMEM((te, tm), BF16), pltpu.VMEM((te, tm), BF16),
                        pltpu.VMEM((P_H, P_NK, tm), BF16),
                        pltpu.VMEM((P_H, P_NK, tm), BF16)],
        compiler_params=_cparams(("parallel", "arbitrary")),
        name=name,
    )(h2t, x1, u, vt, r2t, e2t, cntt, ft)


def _swap_halves(t, axis=-1):
    a, b = jnp.split(t, 2, axis=axis)
    return jnp.concatenate([b, a], axis=axis)


def _pad_cols(t, n):
    return jnp.pad(t, ((0, 0), (0, n - t.shape[1])))


def _layer(x2, pos_col, B, S, attn_norm_g, w_in, b_igate, b_fgate, mlstm_norm_g, cq_norm_g, w_uq,
           ckv_norm_g, w_ukv, q_norm_g, k_norm_g, w_out, ffn_norm_g, peer_w_q, peer_keys_1,
           peer_keys_2, peer_u, peer_v):
    T = B * S
    n_qkvo = 4 * M_H * M_DK
    o_gate = n_qkvo
    o_cq = o_gate + 2 * M_H
    o_ckv = o_cq + A_QR
    o_kr = o_ckv + A_KVR

    w_main = w_in[:, :n_qkvo].astype(BF16)
    w_kr = w_in[:, o_kr:o_kr + A_ROPE]
    w_lat = jnp.concatenate([w_in[:, o_cq:o_kr], _pad_cols(w_kr, 128),
                             _pad_cols(_swap_halves(w_kr), 128)], axis=1).astype(BF16)
    w_gate = _pad_cols(w_in[:, o_gate:o_gate + 2 * M_H], 128).astype(BF16)
    bias_row = _pad_cols(jnp.concatenate([b_igate, b_fgate])[None, :], 128)

    wq_h = w_uq.reshape(A_QR, A_H, A_DQK)
    wq_pad = jnp.pad(wq_h, ((0, 0), (0, 0), (0, A_DPAD - A_DQK))).reshape(A_QR, A_H * A_DPAD).astype(BF16)
    wq_sw = jnp.pad(_swap_halves(wq_h[:, :, A_NOPE:]), ((0, 0), (0, 0), (0, 128 - A_ROPE)))
    wq_sw = wq_sw.reshape(A_QR, A_H * 128).astype(BF16)
    w_kv = w_ukv.astype(BF16)
    qg = _pad_cols(q_norm_g[None, :], A_DPAD)
    qgs = _pad_cols(_swap_halves(q_norm_g[A_NOPE:])[None, :], 128)
    kg = _pad_cols(k_norm_g[None, :], A_DPAD)
    kgs = _pad_cols(_swap_halves(k_norm_g[A_NOPE:])[None, :], 128)
    inv_freq = ROPE_THETA ** (-jnp.arange(0, A_ROPE, 2, dtype=F32) / A_ROPE)
    freq_row = _pad_cols(jnp.concatenate([inv_freq, inv_freq])[None, :], 128)
    half = A_ROPE // 2
    sign_row = _pad_cols(jnp.concatenate([-jnp.ones((half,), F32), jnp.ones((half,), F32)])[None, :], 128)

    qkvo, lat, gates = _in_proj(x2, attn_norm_g[None, :], w_main, w_lat, w_gate, tm=min(512, T))
    hb = B // 2
    if hb:
        y_m = jnp.concatenate([
            _mlstm(qkvo[:hb * S], gates[:hb * S], bias_row, mlstm_norm_g[None, :], hb, S, L=256, name="mlstm_a"),
            _mlstm(qkvo[hb * S:], gates[hb * S:], bias_row, mlstm_norm_g[None, :], B - hb, S, L=128,
                   name="mlstm_b")], axis=0)
    else:
        y_m = _mlstm(qkvo, gates, bias_row, mlstm_norm_g[None, :], B, S, L=min(256, S))
    consts = [cq_norm_g[None, :], ckv_norm_g[None, :], wq_pad, wq_sw, w_kv, qg, qgs, kg, kgs,
              freq_row, sign_row]
    q, k, v = _mla_prep(lat, pos_col, consts, tm=min(512, T))
    if hb:
        y_a = jnp.concatenate([
            _mla_attn(q[:hb * S], k[:hb * S], v[:hb * S], hb, S, tq=512, name="mla_attn_a"),
            _mla_attn(q[hb * S:], k[hb * S:], v[hb * S:], B - hb, S, tq=256, name="mla_attn_b")], axis=0)
    else:
        y_a = _mla_attn(q, k, v, B, S, tq=min(512, S))
    wo = w_out.astype(BF16)
    x1, h2t, s1t, s2t = _out_proj(x2, y_m, y_a, wo[:M_H * M_DV], wo[M_H * M_DV:], ffn_norm_g[None, :],
                                  peer_w_q.astype(BF16), peer_keys_1.astype(BF16),
                                  peer_keys_2.astype(BF16), tm=min(512, T))
    r2t, e2t, cntt, ft = _peer_route(s1t, s2t, tl=min(256, T))
    u_pk = _pack_rows(peer_u.astype(BF16))
    vt_pk = _pack_rows(peer_v.astype(BF16).T)
    variants = [functools.partial(_peer_mlp, nb=8, name="peer_mlp_a"),
                functools.partial(_peer_mlp, nb=16, name="peer_mlp_b"),
                functools.partial(_peer_mlp_p, nb=8, name="peer_mlp_c"),
                functools.partial(_peer_mlp, nb=8, fold_half=True, name="peer_mlp_d")]
    qt = T // len(variants)
    outs = []
    for i, fn in enumerate(variants):
        sl = slice(i * qt, (i + 1) * qt)
        outs.append(fn(h2t[:, sl], x1[sl], u_pk, vt_pk, r2t[:, :, sl], e2t[:, :, sl], cntt[:, :, sl],
                       ft[:, :, sl], tm=min(512, qt)))
    return jnp.concatenate(outs, axis=0)


def kernel(x, positions, attn_norm_g, w_in, b_igate, b_fgate, mlstm_norm_g, cq_norm_g, w_uq, ckv_norm_g,
           w_ukv, q_norm_g, k_norm_g, w_out, ffn_norm_g, peer_w_q, peer_keys_1, peer_keys_2, peer_u, peer_v):
    B, S, D = x.shape
    x2 = x.reshape(B * S, D)
    pos_col = positions.reshape(B * S, 1).astype(F32)
    for l in range(attn_norm_g.shape[0]):
        x2 = _layer(x2, pos_col, B, S, attn_norm_g[l], w_in[l], b_igate[l], b_fgate[l], mlstm_norm_g[l],
                    cq_norm_g[l], w_uq[l], ckv_norm_g[l], w_ukv[l], q_norm_g[l], k_norm_g[l], w_out[l],
                    ffn_norm_g[l], peer_w_q[l], peer_keys_1[l], peer_keys_2[l], peer_u[l], peer_v[l])
    return x2.reshape(B, S, D)
```

```python
import functools

import jax
import jax.numpy as jnp
from jax import lax
from jax.experimental import pallas as pl
from jax.experimental.pallas import tpu as pltpu

F32 = jnp.float32
BF16 = jnp.bfloat16

D_MODEL = 1024
RMS_EPS = 1e-6
M_H, M_DK, M_DV = 4, 128, 128
A_H, A_QR, A_KVR, A_NOPE, A_ROPE, A_DV = 4, 256, 128, 128, 64, 128
A_DQK = A_NOPE + A_ROPE
A_DPAD = 256
ROPE_THETA = 10000.0
P_H, P_NK, P_DK, P_TOPK = 8, 128, 256, 16
NEG_INF = float("-inf")

VMEM_LIMIT = 56 * 1024 * 1024


def _cparams(sem):
    return pltpu.CompilerParams(dimension_semantics=sem, vmem_limit_bytes=VMEM_LIMIT)


def _dot(a, b):
    return jnp.dot(a, b, preferred_element_type=F32)


def _dot_nt(a, b):
    return lax.dot_general(a, b, (((1,), (1,)), ((), ())), preferred_element_type=F32)


def _dot_tn(a, b):
    return lax.dot_general(a, b, (((0,), (0,)), ((), ())), preferred_element_type=F32)


def _inproj_kernel(x_ref, g_ref, wm_ref, wl_ref, wg_ref, qkvo_ref, lat_ref, gate_ref):
    x = x_ref[...]
    ms = jnp.mean(x * x, axis=-1, keepdims=True)
    h = (x * lax.rsqrt(ms + RMS_EPS) * g_ref[...]).astype(BF16)
    main = _dot(h, wm_ref[...])
    kcol = lax.broadcasted_iota(jnp.int32, (1, main.shape[1]), 1) // (M_H * M_DK) == 1
    main = jnp.where(kcol, main * (M_DK ** -0.5), main)
    qkvo_ref[...] = main.astype(BF16)
    lat_ref[...] = _dot(h, wl_ref[...])
    gate_ref[...] = _dot(h, wg_ref[...])


def _in_proj(x2, g, w_main, w_lat, w_gate, tm):
    T = x2.shape[0]
    n_main, n_lat, n_gate = w_main.shape[1], w_lat.shape[1], w_gate.shape[1]
    full = lambda shape: pl.BlockSpec(shape, lambda i: (0, 0))
    return pl.pallas_call(
        _inproj_kernel,
        grid=(T // tm,),
        in_specs=[pl.BlockSpec((tm, D_MODEL), lambda i: (i, 0)), full((1, D_MODEL)),
                  full(w_main.shape), full(w_lat.shape), full(w_gate.shape)],
        out_specs=[pl.BlockSpec((tm, n_main), lambda i: (i, 0)),
                   pl.BlockSpec((tm, n_lat), lambda i: (i, 0)),
                   pl.BlockSpec((tm, n_gate), lambda i: (i, 0))],
        out_shape=[jax.ShapeDtypeStruct((T, n_main), BF16),
                   jax.ShapeDtypeStruct((T, n_lat), F32),
                   jax.ShapeDtypeStruct((T, n_gate), F32)],
        compiler_params=_cparams(("parallel",)),
        name="in_proj",
    )(x2, g, w_main, w_lat, w_gate)


def _log_sigmoid(x):
    return -(jnp.maximum(-x, 0.0) + jnp.log1p(jnp.exp(-jnp.abs(x))))


def _mlstm_kernel(qkvo_ref, gate_ref, bias_ref, ng_ref, y_ref, c_ref, m_ref, *, L):
    c_idx = pl.program_id(1)

    @pl.when(c_idx == 0)
    def _():
        c_ref[...] = jnp.zeros_like(c_ref)
        m_ref[...] = jnp.zeros_like(m_ref)

    H, DK, DV = M_H, M_DK, M_DV
    G = gate_ref[...] + bias_ref[...]
    LF = _log_sigmoid(G)
    row = lax.broadcasted_iota(jnp.int32, (L, L), 0)
    col = lax.broadcasted_iota(jnp.int32, (L, L), 1)
    causal = col <= row
    tri = jnp.where(causal, 1.0, 0.0)
    Bc = jnp.dot(tri, LF, preferred_element_type=F32, precision=lax.Precision.HIGHEST)
    lane = lax.broadcasted_iota(jnp.int32, (L, 128), 1)
    ZT = jnp.where(lane < H, G, Bc).T
    ones_col = jnp.where(lax.broadcasted_iota(jnp.int32, (L, 128), 1) == 0, 1.0, 0.0).astype(BF16)

    for h in range(H):
        q = qkvo_ref[:, h * DK:(h + 1) * DK]
        k = qkvo_ref[:, H * DK + h * DK: H * DK + (h + 1) * DK]
        v = qkvo_ref[:, 2 * H * DK + h * DV: 2 * H * DK + (h + 1) * DV]
        o = qkvo_ref[:, 3 * H * DK + h * DV: 3 * H * DK + (h + 1) * DV]
        i_col = G[:, h:h + 1]
        b_col = Bc[:, H + h:H + h + 1]
        i_row = ZT[h:h + 1, :]
        b_row = ZT[H + h:H + h + 1, :]
        m_prev = m_ref[h]
        c_prev = c_ref[h]

        a_col = b_col + m_prev
        D = jnp.where(causal, b_col - b_row + i_row, NEG_INF)
        m_t = jnp.maximum(a_col, jnp.max(D, axis=-1, keepdims=True))
        Dw = jnp.exp(D - m_t)
        aw = jnp.exp(a_col - m_t)
        s = (_dot_nt(q, k) * Dw).astype(BF16)
        v_aug = jnp.concatenate([v, ones_col], axis=-1)
        num = aw * _dot(q, c_prev.astype(BF16)) + _dot(s, v_aug)
        den = num[:, DV:DV + 1]
        hval = num[:, :DV] / jnp.maximum(jnp.abs(den), jnp.exp(-m_t))

        bL = b_col[L - 1:L, :]
        g_col = bL - b_col + i_col
        m_new = jnp.maximum(bL + m_prev, jnp.max(g_col, axis=0, keepdims=True))
        decay = jnp.exp(bL + m_prev - m_new)
        w_col = jnp.exp(g_col - m_new)
        wv = (w_col * v_aug.astype(F32)).astype(BF16)
        c_ref[h] = decay * c_prev + _dot_tn(k, wv)
        m_ref[h] = m_new

        ms = jnp.mean(hval * hval, axis=-1, keepdims=True)
        hn = hval * lax.rsqrt(ms + RMS_EPS) * ng_ref[:, h * DV:(h + 1) * DV]
        y_ref[:, h * DV:(h + 1) * DV] = (jax.nn.sigmoid(o.astype(F32)) * hn).astype(BF16)


def _mlstm(qkvo, gates, bias_row, norm_g, B, S, L):
    T = B * S
    nc = S // L
    return pl.pallas_call(
        functools.partial(_mlstm_kernel, L=L),
        grid=(B, nc),
        in_specs=[pl.BlockSpec((L, qkvo.shape[1]), lambda b, c: (b * nc + c, 0)),
                  pl.BlockSpec((L, 128), lambda b, c: (b * nc + c, 0)),
                  pl.BlockSpec((1, 128), lambda b, c: (0, 0)),
                  pl.BlockSpec((1, M_H * M_DV), lambda b, c: (0, 0))],
        out_specs=pl.BlockSpec((L, M_H * M_DV), lambda b, c: (b * nc + c, 0)),
        out_shape=jax.ShapeDtypeStruct((T, M_H * M_DV), BF16),
        scratch_shapes=[pltpu.VMEM((M_H, M_DK, 2 * M_DV), F32),
                        pltpu.VMEM((M_H, 1, 1), F32)],
        compiler_params=_cparams(("parallel", "arbitrary")),
        name="mlstm",
    )(qkvo, gates, bias_row, norm_g)


def _mla_prep_kernel(lat_ref, pos_ref, cqg_ref, ckvg_ref, wq_ref, wqs_ref, wkv_ref,
                     qg_ref, qgs_ref, kg_ref, kgs_ref, freq_ref, sign_ref,
                     q_ref, k_ref, v_ref):
    H = A_H
    cq = lat_ref[:, :A_QR]
    ckv = lat_ref[:, A_QR:A_QR + A_KVR]
    kr = lat_ref[:, A_QR + A_KVR:A_QR + A_KVR + 128]
    krs = lat_ref[:, A_QR + A_KVR + 128:A_QR + A_KVR + 256]

    def rms(t, g):
        return t * lax.rsqrt(jnp.mean(t * t, axis=-1, keepdims=True) + RMS_EPS) * g

    cqn = rms(cq, cqg_ref[...]).astype(BF16)
    ckvn = rms(ckv, ckvg_ref[...]).astype(BF16)
    qf = _dot(cqn, wq_ref[...])
    qs = _dot(cqn, wqs_ref[...])
    kv = _dot(ckvn, wkv_ref[...])

    ang = pos_ref[...] * freq_ref[...]
    cos = jnp.cos(ang)
    sin = jnp.sin(ang) * sign_ref[...]
    scale = A_DQK ** -0.5
    kr_ss = jnp.sum(kr * kr, axis=-1, keepdims=True)

    for h in range(H):
        qh = qf[:, h * A_DPAD:(h + 1) * A_DPAD]
        rstd = lax.rsqrt(jnp.sum(qh * qh, axis=-1, keepdims=True) / A_DQK + RMS_EPS) * scale
        q_nope = qh[:, :128] * rstd * qg_ref[:, :128]
        q_rope = qh[:, 128:] * rstd * qg_ref[:, 128:]
        q_rope_s = qs[:, h * 128:(h + 1) * 128] * rstd * qgs_ref[...]
        q_ref[:, h * A_DPAD:h * A_DPAD + 128] = q_nope.astype(BF16)
        q_ref[:, h * A_DPAD + 128:(h + 1) * A_DPAD] = (q_rope * cos + q_rope_s * sin).astype(BF16)

        kn = kv[:, h * 256:h * 256 + 128]
        rstd_k = lax.rsqrt((jnp.sum(kn * kn, axis=-1, keepdims=True) + kr_ss) / A_DQK + RMS_EPS)
        k_nope = kn * rstd_k * kg_ref[:, :128]
        k_rope = kr * rstd_k * kg_ref[:, 128:]
        k_rope_s = krs * rstd_k * kgs_ref[...]
        k_ref[:, h * A_DPAD:h * A_DPAD + 128] = k_nope.astype(BF16)
        k_ref[:, h * A_DPAD + 128:(h + 1) * A_DPAD] = (k_rope * cos + k_rope_s * sin).astype(BF16)
        v_ref[:, h * A_DV:(h + 1) * A_DV] = kv[:, h * 256 + 128:(h + 1) * 256].astype(BF16)


def _mla_prep(lat, pos_col, consts, tm):
    T = lat.shape[0]
    full = lambda a: pl.BlockSpec(a.shape, lambda i: (0, 0))
    return pl.pallas_call(
        _mla_prep_kernel,
        grid=(T // tm,),
        in_specs=[pl.BlockSpec((tm, lat.shape[1]), lambda i: (i, 0)),
                  pl.BlockSpec((tm, 1), lambda i: (i, 0))] + [full(c) for c in consts],
        out_specs=[pl.BlockSpec((tm, A_H * A_DPAD), lambda i: (i, 0)),
                   pl.BlockSpec((tm, A_H * A_DPAD), lambda i: (i, 0)),
                   pl.BlockSpec((tm, A_H * A_DV), lambda i: (i, 0))],
        out_shape=[jax.ShapeDtypeStruct((T, A_H * A_DPAD), BF16),
                   jax.ShapeDtypeStruct((T, A_H * A_DPAD), BF16),
                   jax.ShapeDtypeStruct((T, A_H * A_DV), BF16)],
        compiler_params=_cparams(("parallel",)),
        name="mla_prep",
    )(lat, pos_col, *consts)


def _attn_kernel(q_ref, k_ref, v_ref, o_ref, *, tq, hp):
    qi = pl.program_id(2)

    def chunk(j, carry, masked):
        rows = pl.ds(pl.multiple_of(j * tq, tq), tq)
        out = []
        for h in range(hp):
            m, l, acc = carry[h]
            q = q_ref[:, h * A_DPAD:(h + 1) * A_DPAD]
            k = k_ref[rows, h * A_DPAD:(h + 1) * A_DPAD]
            v = v_ref[rows, h * A_DV:(h + 1) * A_DV]
            s = _dot_nt(q, k)
            if masked:
                row = lax.broadcasted_iota(jnp.int32, (tq, tq), 0)
                col = lax.broadcasted_iota(jnp.int32, (tq, tq), 1)
                s = jnp.where(col <= row, s, NEG_INF)
            m_new = jnp.maximum(m, jnp.max(s, axis=-1, keepdims=True))
            alpha = jnp.exp(m - m_new)
            p = jnp.exp(s - m_new)
            l = alpha * l + jnp.sum(p, axis=-1, keepdims=True)
            acc = alpha * acc + _dot(p.astype(BF16), v)
            out.append((m_new, l, acc))
        return tuple(out)

    init = tuple((jnp.full((tq, 1), NEG_INF, F32), jnp.zeros((tq, 1), F32), jnp.zeros((tq, A_DV), F32))
                 for _ in range(hp))
    carry = lax.fori_loop(0, qi, lambda j, c: chunk(j, c, False), init)
    carry = chunk(qi, carry, True)
    for h in range(hp):
        m, l, acc = carry[h]
        o_ref[:, h * A_DV:(h + 1) * A_DV] = (acc / l).astype(BF16)


def _mla_attn(q, k, v, B, S, tq, hp=1, name="mla_attn"):
    T = B * S
    nq = S // tq
    return pl.pallas_call(
        functools.partial(_attn_kernel, tq=tq, hp=hp),
        grid=(B, A_H // hp, nq),
        in_specs=[pl.BlockSpec((tq, hp * A_DPAD), lambda b, h, i: (b * nq + i, h)),
                  pl.BlockSpec((S, hp * A_DPAD), lambda b, h, i: (b, h)),
                  pl.BlockSpec((S, hp * A_DV), lambda b, h, i: (b, h))],
        out_specs=pl.BlockSpec((tq, hp * A_DV), lambda b, h, i: (b * nq + i, h)),
        out_shape=jax.ShapeDtypeStruct((T, A_H * A_DV), BF16),
        compiler_params=_cparams(("parallel", "parallel", "arbitrary")),
        name=name,
    )(q, k, v)


def _outproj_kernel(x_ref, ym_ref, ya_ref, wom_ref, woa_ref, g_ref, wq_ref, k1_ref, k2_ref,
                    x1_ref, h2t_ref, s1_ref, s2_ref):
    x1 = x_ref[...] + _dot(ym_ref[...], wom_ref[...]) + _dot(ya_ref[...], woa_ref[...])
    x1_ref[...] = x1
    ms = jnp.mean(x1 * x1, axis=-1, keepdims=True)
    h2f = x1 * lax.rsqrt(ms + RMS_EPS) * g_ref[...]
    h2 = h2f.astype(BF16)
    h2t_ref[...] = pltpu.bitcast(h2f.T.astype(BF16), jnp.uint32)
    qry = _dot(h2, wq_ref[...]).astype(BF16)
    half = P_DK // 2
    for h in range(P_H):
        s1_ref[h] = _dot_nt(k1_ref[...], qry[:, h * P_DK:h * P_DK + half])
        s2_ref[h] = _dot_nt(k2_ref[...], qry[:, h * P_DK + half:(h + 1) * P_DK])


def _out_proj(x2, ym, ya, wo_m, wo_a, g, wq, k1, k2, tm):
    T = x2.shape[0]
    full = lambda a: pl.BlockSpec(a.shape, lambda i: (0, 0))
    return pl.pallas_call(
        _outproj_kernel,
        grid=(T // tm,),
        in_specs=[pl.BlockSpec((tm, D_MODEL), lambda i: (i, 0)),
                  pl.BlockSpec((tm, ym.shape[1]), lambda i: (i, 0)),
                  pl.BlockSpec((tm, ya.shape[1]), lambda i: (i, 0)),
                  full(wo_m), full(wo_a), full(g), full(wq), full(k1), full(k2)],
        out_specs=[pl.BlockSpec((tm, D_MODEL), lambda i: (i, 0)),
                   pl.BlockSpec((D_MODEL // 2, tm), lambda i: (0, i)),
                   pl.BlockSpec((P_H, P_NK, tm), lambda i: (0, 0, i)),
                   pl.BlockSpec((P_H, P_NK, tm), lambda i: (0, 0, i))],
        out_shape=[jax.ShapeDtypeStruct((T, D_MODEL), F32),
                   jax.ShapeDtypeStruct((D_MODEL // 2, T), jnp.uint32),
                   jax.ShapeDtypeStruct((P_H, P_NK, T), F32),
                   jax.ShapeDtypeStruct((P_H, P_NK, T), F32)],
        compiler_params=_cparams(("parallel",)),
        name="out_peerq",
    )(x2, ym, ya, wo_m, wo_a, g, wq, k1, k2)


def _top16(x):
    vals = []
    for _ in range(P_TOPK):
        m = jnp.max(x, axis=0, keepdims=True)
        vals.append(m)
        x = jnp.where(x == m, NEG_INF, x)
    return vals


def _oddeven_merge(lo, hi, r):
    step = r * 2
    if step < hi - lo:
        yield from _oddeven_merge(lo, hi, step)
        yield from _oddeven_merge(lo + r, hi, step)
        yield from [(i, i + r) for i in range(lo + r, hi - r, step)]
    else:
        yield (lo, lo + r)


def _oddeven_merge_sort(lo, hi):
    if hi - lo >= 1:
        mid = lo + (hi - lo) // 2
        yield from _oddeven_merge_sort(lo, mid)
        yield from _oddeven_merge_sort(mid + 1, hi)
        yield from _oddeven_merge(lo, hi, 1)


_SORT16 = tuple(_oddeven_merge_sort(0, P_TOPK - 1))
_SUBLANES = 8


def _cmpx(v, i, j):
    a, b = v[i], v[j]
    v[i] = jnp.maximum(a, b)
    v[j] = jnp.minimum(a, b)


def _top16_sorted(x):
    v = [x[i * _SUBLANES:(i + 1) * _SUBLANES, :] for i in range(P_TOPK)]
    for i, j in _SORT16:
        _cmpx(v, i, j)
    for shift in (4, 2, 1):
        o = [pltpu.roll(t, shift, axis=0) for t in v]
        v = [jnp.maximum(v[i], o[P_TOPK - 1 - i]) for i in range(P_TOPK)]
        for d in (8, 4, 2, 1):
            for i in range(P_TOPK):
                if not i & d:
                    _cmpx(v, i, i + d)
    return v


_CAND = [(a, b) for a in range(P_TOPK) for b in range(P_TOPK) if (a + 1) * (b + 1) <= P_TOPK]


def _route_kernel(s1_ref, s2_ref, r2_ref, e2_ref, cnt_ref, f_ref):
    s1 = s1_ref[0]
    s2 = s2_ref[0]
    ng = P_NK // _SUBLANES
    v1s = _top16_sorted(s1)
    v2s = _top16_sorted(s2)
    v1 = [t[0:1, :] for t in v1s]
    v2 = [t[0:1, :] for t in v2s]
    cand = jnp.concatenate([v1[a] + v2[b] for a, b in _CAND], axis=0)
    tau = _top16(cand)[P_TOPK - 1]
    top = v1[0] + v2[0]
    z = jnp.zeros_like(tau)
    cnt_a = [jnp.zeros_like(tau) for _ in range(P_TOPK)]
    for a, b in _CAND:
        sm = v1[a] + v2[b]
        sel = sm >= tau
        z = z + jnp.where(sel, jnp.exp(sm - top), 0.0)
        cnt_a[a] = cnt_a[a] + jnp.where(sel, 1.0, 0.0)
    cnt_b = [jnp.broadcast_to(c, (_SUBLANES, c.shape[1])) for c in cnt_a]
    for i in range(ng):
        rows = slice(i * _SUBLANES, (i + 1) * _SUBLANES)
        x1, x2 = s1[rows, :], s2[rows, :]
        r2 = jnp.full_like(x2, float(P_TOPK))
        cnt = jnp.zeros_like(x1)
        for a in reversed(range(P_TOPK)):
            r2 = jnp.where(x2 >= v2s[a], float(a), r2)
            cnt = jnp.where(x1 == v1s[a], cnt_b[a], cnt)
        r2_ref[0, rows, :] = r2
        cnt_ref[0, rows, :] = cnt
    e2_ref[0] = jnp.exp(s2 - v2[0])
    f_ref[0] = jnp.exp(s1 - v1[0]) / z


def _peer_route(s1t, s2t, tl):
    T = s1t.shape[2]
    spec = pl.BlockSpec((1, P_NK, tl), lambda h, i: (h, 0, i))
    shp32 = jax.ShapeDtypeStruct((P_H, P_NK, T), F32)
    return pl.pallas_call(
        _route_kernel,
        grid=(P_H, T // tl),
        in_specs=[spec, spec],
        out_specs=[spec, spec, spec, spec],
        out_shape=[shp32, shp32, shp32, shp32],
        compiler_params=_cparams(("parallel", "parallel")),
        name="peer_route",
    )(s1t, s2t)


def _gelu2(x):
    return x * (1.0 + lax.erf(x * (2.0 ** -0.5)))


_LANES = 128
_PACK = 16
_N1_GROUP = 4


def _peer_kernel(h2t_ref, x1_ref, u_ref, vt_ref, r2f_ref, e2f_ref, cnt_ref, f_ref, o_ref,
                 acc_ref, a_ref, p_ref, r2_ref, e2_ref, *, nb, tm):
    e = pl.program_id(1)

    @pl.when(e == 0)
    def _():
        acc_ref[...] = jnp.zeros_like(acc_ref)
        for h in range(P_H):
            r2_ref[h] = r2f_ref[h].astype(BF16)
            e2_ref[h] = e2f_ref[h].astype(BF16)

    a_ref[...] = _dot(pltpu.bitcast(u_ref[...], BF16), pltpu.bitcast(h2t_ref[...], BF16))
    n1_0 = pl.multiple_of(e * nb, nb)
    zero = jnp.zeros((_PACK, _LANES), BF16)
    for g in range(tm // _LANES):
        lanes = slice(g * _LANES, (g + 1) * _LANES)
        for jc in range(nb // _N1_GROUP):
            w = [[None] * (P_NK // _PACK) for _ in range(_N1_GROUP)]
            for h in range(P_H):
                cnt8 = cnt_ref[h, pl.ds(n1_0, nb), lanes]
                f8 = 0.5 * f_ref[h, pl.ds(n1_0, nb), lanes]
                r2t = [r2_ref[h, r * _PACK:(r + 1) * _PACK, lanes] for r in range(P_NK // _PACK)]
                e2t = [e2_ref[h, r * _PACK:(r + 1) * _PACK, lanes] for r in range(P_NK // _PACK)]
                for jj in range(_N1_GROUP):
                    j = jc * _N1_GROUP + jj
                    cb = jnp.broadcast_to(cnt8[j:j + 1, :], (_PACK, _LANES)).astype(BF16)
                    fb = jnp.broadcast_to(f8[j:j + 1, :], (_PACK, _LANES)).astype(BF16)
                    for r in range(P_NK // _PACK):
                        t = jnp.where(r2t[r] < cb, e2t[r], zero) * fb
                        w[jj][r] = t if w[jj][r] is None else w[jj][r] + t
            for jj in range(_N1_GROUP):
                j = jc * _N1_GROUP + jj
                for r in range(P_NK // _PACK):
                    rows = slice(j * P_NK + r * _PACK, j * P_NK + (r + 1) * _PACK)
                    p_ref[rows, lanes] = w[jj][r] * _gelu2(a_ref[rows, lanes]).astype(BF16)
    acc_ref[...] += _dot(pltpu.bitcast(vt_ref[...], BF16), p_ref[...])

    @pl.when(e == pl.num_programs(1) - 1)
    def _():
        o_ref[...] = x1_ref[...] + acc_ref[...].T


def _peer_mlp(h2t, x1, u, vt, r2t, e2t, cntt, ft, tm, nb):
    T = x1.shape[0]
    ne = P_NK // nb
    te = nb * P_NK
    rspec = pl.BlockSpec((P_H, P_NK, tm), lambda i, e: (0, 0, i))
    return pl.pallas_call(
        functools.partial(_peer_kernel, nb=nb, tm=tm),
        grid=(T // tm, ne),
        in_specs=[pl.BlockSpec((D_MODEL // 2, tm), lambda i, e: (0, i)),
                  pl.BlockSpec((tm, D_MODEL), lambda i, e: (i, 0)),
                  pl.BlockSpec((te // 2, D_MODEL), lambda i, e: (e, 0)),
                  pl.BlockSpec((D_MODEL // 2, te), lambda i, e: (0, e)),
                  rspec, rspec, rspec, rspec],
        out_specs=pl.BlockSpec((tm, D_MODEL), lambda i, e: (i, 0)),
        out_shape=jax.ShapeDtypeStruct((T, D_MODEL), F32),
        scratch_shapes=[pltpu.VMEM((D_MODEL, tm), F32),
                        pltpu.VMEM((te, tm), F32),
                        pltpu.VMEM((te, tm), BF16),
                        pltpu.VMEM((P_H, P_NK, tm), BF16),
                        pltpu.VMEM((P_H, P_NK, tm), BF16)],
        compiler_params=_cparams(("parallel", "arbitrary")),
        name="peer_mlp",
    )(h2t, x1, u, vt, r2t, e2t, cntt, ft)


def _pack_rows(t):
    r, c = t.shape
    return lax.bitcast_convert_type(t.reshape(r // 2, 2, c).swapaxes(1, 2), jnp.uint32)


def _swap_halves(t, axis=-1):
    a, b = jnp.split(t, 2, axis=axis)
    return jnp.concatenate([b, a], axis=axis)


def _pad_cols(t, n):
    return jnp.pad(t, ((0, 0), (0, n - t.shape[1])))


def _layer(x2, pos_col, B, S, attn_norm_g, w_in, b_igate, b_fgate, mlstm_norm_g, cq_norm_g, w_uq,
           ckv_norm_g, w_ukv, q_norm_g, k_norm_g, w_out, ffn_norm_g, peer_w_q, peer_keys_1,
           peer_keys_2, peer_u, peer_v):
    T = B * S
    n_qkvo = 4 * M_H * M_DK
    o_gate = n_qkvo
    o_cq = o_gate + 2 * M_H
    o_ckv = o_cq + A_QR
    o_kr = o_ckv + A_KVR

    w_main = w_in[:, :n_qkvo].astype(BF16)
    w_kr = w_in[:, o_kr:o_kr + A_ROPE]
    w_lat = jnp.concatenate([w_in[:, o_cq:o_kr], _pad_cols(w_kr, 128),
                             _pad_cols(_swap_halves(w_kr), 128)], axis=1).astype(BF16)
    w_gate = _pad_cols(w_in[:, o_gate:o_gate + 2 * M_H], 128).astype(BF16)
    bias_row = _pad_cols(jnp.concatenate([b_igate, b_fgate])[None, :], 128)

    wq_h = w_uq.reshape(A_QR, A_H, A_DQK)
    wq_pad = jnp.pad(wq_h, ((0, 0), (0, 0), (0, A_DPAD - A_DQK))).reshape(A_QR, A_H * A_DPAD).astype(BF16)
    wq_sw = jnp.pad(_swap_halves(wq_h[:, :, A_NOPE:]), ((0, 0), (0, 0), (0, 128 - A_ROPE)))
    wq_sw = wq_sw.reshape(A_QR, A_H * 128).astype(BF16)
    w_kv = w_ukv.astype(BF16)
    qg = _pad_cols(q_norm_g[None, :], A_DPAD)
    qgs = _pad_cols(_swap_halves(q_norm_g[A_NOPE:])[None, :], 128)
    kg = _pad_cols(k_norm_g[None, :], A_DPAD)
    kgs = _pad_cols(_swap_halves(k_norm_g[A_NOPE:])[None, :], 128)
    inv_freq = ROPE_THETA ** (-jnp.arange(0, A_ROPE, 2, dtype=F32) / A_ROPE)
    freq_row = _pad_cols(jnp.concatenate([inv_freq, inv_freq])[None, :], 128)
    half = A_ROPE // 2
    sign_row = _pad_cols(jnp.concatenate([-jnp.ones((half,), F32), jnp.ones((half,), F32)])[None, :], 128)

    qkvo, lat, gates = _in_proj(x2, attn_norm_g[None, :], w_main, w_lat, w_gate, tm=min(512, T))
    y_m = _mlstm(qkvo, gates, bias_row, mlstm_norm_g[None, :], B, S, L=min(256, S))
    consts = [cq_norm_g[None, :], ckv_norm_g[None, :], wq_pad, wq_sw, w_kv, qg, qgs, kg, kgs,
              freq_row, sign_row]
    q, k, v = _mla_prep(lat, pos_col, consts, tm=min(512, T))
    hb = B // 2
    if hb:
        y_a = jnp.concatenate([
            _mla_attn(q[:hb * S], k[:hb * S], v[:hb * S], hb, S, tq=512, hp=2, name="mla_attn_a"),
            _mla_attn(q[hb * S:], k[hb * S:], v[hb * S:], B - hb, S, tq=512, hp=4, name="mla_attn_b")], axis=0)
    else:
        y_a = _mla_attn(q, k, v, B, S, tq=min(512, S))
    wo = w_out.astype(BF16)
    x1, h2t, s1t, s2t = _out_proj(x2, y_m, y_a, wo[:M_H * M_DV], wo[M_H * M_DV:], ffn_norm_g[None, :],
                                  peer_w_q.astype(BF16), peer_keys_1.astype(BF16),
                                  peer_keys_2.astype(BF16), tm=min(512, T))
    r2t, e2t, cntt, ft = _peer_route(s1t, s2t, tl=min(256, T))
    u_pk = _pack_rows(peer_u.astype(BF16))
    vt_pk = _pack_rows(peer_v.astype(BF16).T)
    return _peer_mlp(h2t, x1, u_pk, vt_pk, r2t, e2t, cntt, ft, tm=min(512, T), nb=8)


def kernel(x, positions, attn_norm_g, w_in, b_igate, b_fgate, mlstm_norm_g, cq_norm_g, w_uq, ckv_norm_g,
           w_ukv, q_norm_g, k_norm_g, w_out, ffn_norm_g, peer_w_q, peer_keys_1, peer_keys_2, peer_u, peer_v):
    B, S, D = x.shape
    x2 = x.reshape(B * S, D)
    pos_col = positions.reshape(B * S, 1).astype(F32)
    for l in range(attn_norm_g.shape[0]):
        x2 = _layer(x2, pos_col, B, S, attn_norm_g[l], w_in[l], b_igate[l], b_fgate[l], mlstm_norm_g[l],
                    cq_norm_g[l], w_uq[l], ckv_norm_g[l], w_ukv[l], q_norm_g[l], k_norm_g[l], w_out[l],
                    ffn_norm_g[l], peer_w_q[l], peer_keys_1[l], peer_keys_2[l], peer_u[l], peer_v[l])
    return x2.reshape(B, S, D)
```

```python
import functools

import jax
import jax.numpy as jnp
from jax import lax
from jax.experimental import pallas as pl
from jax.experimental.pallas import tpu as pltpu

F32 = jnp.float32
BF16 = jnp.bfloat16

D_MODEL = 1024
RMS_EPS = 1e-6
M_H, M_DK, M_DV = 4, 128, 128
A_H, A_QR, A_KVR, A_NOPE, A_ROPE, A_DV = 4, 256, 128, 128, 64, 128
A_DQK = A_NOPE + A_ROPE
A_DPAD = 256
ROPE_THETA = 10000.0
P_H, P_NK, P_DK, P_TOPK = 8, 128, 256, 16
NEG_INF = float("-inf")

VMEM_LIMIT = 56 * 1024 * 1024


def _cparams(sem):
    return pltpu.CompilerParams(dimension_semantics=sem, vmem_limit_bytes=VMEM_LIMIT)


def _dot(a, b):
    return jnp.dot(a, b, preferred_element_type=F32)


def _dot_nt(a, b):
    return lax.dot_general(a, b, (((1,), (1,)), ((), ())), preferred_element_type=F32)


def _dot_tn(a, b):
    return lax.dot_general(a, b, (((0,), (0,)), ((), ())), preferred_element_type=F32)


def _inproj_kernel(x_ref, g_ref, wm_ref, wl_ref, wg_ref, qkvo_ref, lat_ref, gate_ref):
    x = x_ref[...]
    ms = jnp.mean(x * x, axis=-1, keepdims=True)
    h = (x * lax.rsqrt(ms + RMS_EPS) * g_ref[...]).astype(BF16)
    main = _dot(h, wm_ref[...])
    kcol = lax.broadcasted_iota(jnp.int32, (1, main.shape[1]), 1) // (M_H * M_DK) == 1
    main = jnp.where(kcol, main * (M_DK ** -0.5), main)
    qkvo_ref[...] = main.astype(BF16)
    lat_ref[...] = _dot(h, wl_ref[...])
    gate_ref[...] = _dot(h, wg_ref[...])


def _in_proj(x2, g, w_main, w_lat, w_gate, tm):
    T = x2.shape[0]
    n_main, n_lat, n_gate = w_main.shape[1], w_lat.shape[1], w_gate.shape[1]
    full = lambda shape: pl.BlockSpec(shape, lambda i: (0, 0))
    return pl.pallas_call(
        _inproj_kernel,
        grid=(T // tm,),
        in_specs=[pl.BlockSpec((tm, D_MODEL), lambda i: (i, 0)), full((1, D_MODEL)),
                  full(w_main.shape), full(w_lat.shape), full(w_gate.shape)],
        out_specs=[pl.BlockSpec((tm, n_main), lambda i: (i, 0)),
                   pl.BlockSpec((tm, n_lat), lambda i: (i, 0)),
                   pl.BlockSpec((tm, n_gate), lambda i: (i, 0))],
        out_shape=[jax.ShapeDtypeStruct((T, n_main), BF16),
                   jax.ShapeDtypeStruct((T, n_lat), F32),
                   jax.ShapeDtypeStruct((T, n_gate), F32)],
        compiler_params=_cparams(("parallel",)),
        name="in_proj",
    )(x2, g, w_main, w_lat, w_gate)


def _log_sigmoid(x):
    return -(jnp.maximum(-x, 0.0) + jnp.log1p(jnp.exp(-jnp.abs(x))))


def _mlstm_kernel(qkvo_ref, gate_ref, bias_ref, ng_ref, y_ref, c_ref, m_ref, *, L, nbatch):
    c_idx = pl.program_id(1)

    @pl.when(c_idx == 0)
    def _():
        c_ref[...] = jnp.zeros_like(c_ref)
        m_ref[...] = jnp.zeros_like(m_ref)

    H, DK, DV = M_H, M_DK, M_DV
    row = lax.broadcasted_iota(jnp.int32, (L, L), 0)
    col = lax.broadcasted_iota(jnp.int32, (L, L), 1)
    causal = col <= row
    tri = jnp.where(causal, 1.0, 0.0)
    lane = lax.broadcasted_iota(jnp.int32, (L, 128), 1)
    ones_col = jnp.where(lane == 0, 1.0, 0.0).astype(BF16)

    for bb in range(nbatch):
        G = gate_ref[bb] + bias_ref[...]
        LF = _log_sigmoid(G)
        Bc = jnp.dot(tri, LF, preferred_element_type=F32, precision=lax.Precision.HIGHEST)
        ZT = jnp.where(lane < H, G, Bc).T

        for h in range(H):
            q = qkvo_ref[bb, :, h * DK:(h + 1) * DK]
            k = qkvo_ref[bb, :, H * DK + h * DK: H * DK + (h + 1) * DK]
            v = qkvo_ref[bb, :, 2 * H * DK + h * DV: 2 * H * DK + (h + 1) * DV]
            o = qkvo_ref[bb, :, 3 * H * DK + h * DV: 3 * H * DK + (h + 1) * DV]
            i_col = G[:, h:h + 1]
            b_col = Bc[:, H + h:H + h + 1]
            i_row = ZT[h:h + 1, :]
            b_row = ZT[H + h:H + h + 1, :]
            m_prev = m_ref[bb * H + h]
            c_prev = c_ref[bb * H + h]

            a_col = b_col + m_prev
            D = jnp.where(causal, b_col - b_row + i_row, NEG_INF)
            m_t = jnp.maximum(a_col, jnp.max(D, axis=-1, keepdims=True))
            Dw = jnp.exp(D - m_t)
            aw = jnp.exp(a_col - m_t)
            s = (_dot_nt(q, k) * Dw).astype(BF16)
            v_aug = jnp.concatenate([v, ones_col], axis=-1)
            num = aw * _dot(q, c_prev.astype(BF16)) + _dot(s, v_aug)
            den = num[:, DV:DV + 1]
            hval = num[:, :DV] / jnp.maximum(jnp.abs(den), jnp.exp(-m_t))

            bL = b_col[L - 1:L, :]
            g_col = bL - b_col + i_col
            m_new = jnp.maximum(bL + m_prev, jnp.max(g_col, axis=0, keepdims=True))
            decay = jnp.exp(bL + m_prev - m_new)
            w_col = jnp.exp(g_col - m_new)
            wv = (w_col * v_aug.astype(F32)).astype(BF16)
            c_ref[bb * H + h] = decay * c_prev + _dot_tn(k, wv)
            m_ref[bb * H + h] = m_new

            ms = jnp.mean(hval * hval, axis=-1, keepdims=True)
            hn = hval * lax.rsqrt(ms + RMS_EPS) * ng_ref[:, h * DV:(h + 1) * DV]
            y_ref[bb, :, h * DV:(h + 1) * DV] = (jax.nn.sigmoid(o.astype(F32)) * hn).astype(BF16)


def _mlstm(qkvo, gates, bias_row, norm_g, B, S, L, nbatch):
    nc = S // L
    n_in = qkvo.shape[1]
    y = pl.pallas_call(
        functools.partial(_mlstm_kernel, L=L, nbatch=nbatch),
        grid=(B // nbatch, nc),
        in_specs=[pl.BlockSpec((nbatch, L, n_in), lambda b, c: (b, c, 0)),
                  pl.BlockSpec((nbatch, L, 128), lambda b, c: (b, c, 0)),
                  pl.BlockSpec((1, 128), lambda b, c: (0, 0)),
                  pl.BlockSpec((1, M_H * M_DV), lambda b, c: (0, 0))],
        out_specs=pl.BlockSpec((nbatch, L, M_H * M_DV), lambda b, c: (b, c, 0)),
        out_shape=jax.ShapeDtypeStruct((B, S, M_H * M_DV), BF16),
        scratch_shapes=[pltpu.VMEM((nbatch * M_H, M_DK, 2 * M_DV), F32),
                        pltpu.VMEM((nbatch * M_H, 1, 1), F32)],
        compiler_params=_cparams(("parallel", "arbitrary")),
        name="mlstm",
    )(qkvo.reshape(B, S, n_in), gates.reshape(B, S, 128), bias_row, norm_g)
    return y.reshape(B * S, M_H * M_DV)


def _mla_prep_kernel(lat_ref, pos_ref, cqg_ref, ckvg_ref, wq_ref, wqs_ref, wkv_ref,
                     qg_ref, qgs_ref, kg_ref, kgs_ref, freq_ref, sign_ref,
                     q_ref, k_ref, v_ref):
    H = A_H
    cq = lat_ref[:, :A_QR]
    ckv = lat_ref[:, A_QR:A_QR + A_KVR]
    kr = lat_ref[:, A_QR + A_KVR:A_QR + A_KVR + 128]
    krs = lat_ref[:, A_QR + A_KVR + 128:A_QR + A_KVR + 256]

    def rms(t, g):
        return t * lax.rsqrt(jnp.mean(t * t, axis=-1, keepdims=True) + RMS_EPS) * g

    cqn = rms(cq, cqg_ref[...]).astype(BF16)
    ckvn = rms(ckv, ckvg_ref[...]).astype(BF16)
    qf = _dot(cqn, wq_ref[...])
    qs = _dot(cqn, wqs_ref[...])
    kv = _dot(ckvn, wkv_ref[...])

    ang = pos_ref[...] * freq_ref[...]
    cos = jnp.cos(ang)
    sin = jnp.sin(ang) * sign_ref[...]
    scale = A_DQK ** -0.5
    kr_ss = jnp.sum(kr * kr, axis=-1, keepdims=True)

    for h in range(H):
        qh = qf[:, h * A_DPAD:(h + 1) * A_DPAD]
        rstd = lax.rsqrt(jnp.sum(qh * qh, axis=-1, keepdims=True) / A_DQK + RMS_EPS) * scale
        q_nope = qh[:, :128] * rstd * qg_ref[:, :128]
        q_rope = qh[:, 128:] * rstd * qg_ref[:, 128:]
        q_rope_s = qs[:, h * 128:(h + 1) * 128] * rstd * qgs_ref[...]
        q_ref[:, h * A_DPAD:h * A_DPAD + 128] = q_nope.astype(BF16)
        q_ref[:, h * A_DPAD + 128:(h + 1) * A_DPAD] = (q_rope * cos + q_rope_s * sin).astype(BF16)

        kn = kv[:, h * 256:h * 256 + 128]
        rstd_k = lax.rsqrt((jnp.sum(kn * kn, axis=-1, keepdims=True) + kr_ss) / A_DQK + RMS_EPS)
        k_nope = kn * rstd_k * kg_ref[:, :128]
        k_rope = kr * rstd_k * kg_ref[:, 128:]
        k_rope_s = krs * rstd_k * kgs_ref[...]
        k_ref[:, h * A_DPAD:h * A_DPAD + 128] = k_nope.astype(BF16)
        k_ref[:, h * A_DPAD + 128:(h + 1) * A_DPAD] = (k_rope * cos + k_rope_s * sin).astype(BF16)
        v_ref[:, h * A_DV:(h + 1) * A_DV] = kv[:, h * 256 + 128:(h + 1) * 256].astype(BF16)


def _mla_prep(lat, pos_col, consts, tm):
    T = lat.shape[0]
    full = lambda a: pl.BlockSpec(a.shape, lambda i: (0, 0))
    return pl.pallas_call(
        _mla_prep_kernel,
        grid=(T // tm,),
        in_specs=[pl.BlockSpec((tm, lat.shape[1]), lambda i: (i, 0)),
                  pl.BlockSpec((tm, 1), lambda i: (i, 0))] + [full(c) for c in consts],
        out_specs=[pl.BlockSpec((tm, A_H * A_DPAD), lambda i: (i, 0)),
                   pl.BlockSpec((tm, A_H * A_DPAD), lambda i: (i, 0)),
                   pl.BlockSpec((tm, A_H * A_DV), lambda i: (i, 0))],
        out_shape=[jax.ShapeDtypeStruct((T, A_H * A_DPAD), BF16),
                   jax.ShapeDtypeStruct((T, A_H * A_DPAD), BF16),
                   jax.ShapeDtypeStruct((T, A_H * A_DV), BF16)],
        compiler_params=_cparams(("parallel",)),
        name="mla_prep",
    )(lat, pos_col, *consts)


def _attn_kernel(q_ref, k_ref, v_ref, o_ref, *, tq, hp):
    qi = pl.program_id(2)

    def chunk(j, carry, masked):
        rows = pl.ds(pl.multiple_of(j * tq, tq), tq)
        out = []
        for h in range(hp):
            m, l, acc = carry[h]
            q = q_ref[:, h * A_DPAD:(h + 1) * A_DPAD]
            k = k_ref[rows, h * A_DPAD:(h + 1) * A_DPAD]
            v = v_ref[rows, h * A_DV:(h + 1) * A_DV]
            s = _dot_nt(q, k)
            if masked:
                row = lax.broadcasted_iota(jnp.int32, (tq, tq), 0)
                col = lax.broadcasted_iota(jnp.int32, (tq, tq), 1)
                s = jnp.where(col <= row, s, NEG_INF)
            m_new = jnp.maximum(m, jnp.max(s, axis=-1, keepdims=True))
            alpha = jnp.exp(m - m_new)
            p = jnp.exp(s - m_new)
            l = alpha * l + jnp.sum(p, axis=-1, keepdims=True)
            acc = alpha * acc + _dot(p.astype(BF16), v)
            out.append((m_new, l, acc))
        return tuple(out)

    init = tuple((jnp.full((tq, 1), NEG_INF, F32), jnp.zeros((tq, 1), F32), jnp.zeros((tq, A_DV), F32))
                 for _ in range(hp))
    carry = lax.fori_loop(0, qi, lambda j, c: chunk(j, c, False), init)
    carry = chunk(qi, carry, True)
    for h in range(hp):
        m, l, acc = carry[h]
        o_ref[:, h * A_DV:(h + 1) * A_DV] = (acc / l).astype(BF16)


def _mla_attn(q, k, v, B, S, tq, hp):
    T = B * S
    nq = S // tq
    return pl.pallas_call(
        functools.partial(_attn_kernel, tq=tq, hp=hp),
        grid=(B, A_H // hp, nq),
        in_specs=[pl.BlockSpec((tq, hp * A_DPAD), lambda b, h, i: (b * nq + i, h)),
                  pl.BlockSpec((S, hp * A_DPAD), lambda b, h, i: (b, h)),
                  pl.BlockSpec((S, hp * A_DV), lambda b, h, i: (b, h))],
        out_specs=pl.BlockSpec((tq, hp * A_DV), lambda b, h, i: (b * nq + i, h)),
        out_shape=jax.ShapeDtypeStruct((T, A_H * A_DV), BF16),
        compiler_params=_cparams(("parallel", "parallel", "arbitrary")),
        name="mla_attn",
    )(q, k, v)


def _outproj_kernel(x_ref, ym_ref, ya_ref, wom_ref, woa_ref, g_ref, wq_ref, k1_ref, k2_ref,
                    x1_ref, h2t_ref, s1_ref, s2_ref):
    x1 = x_ref[...] + _dot(ym_ref[...], wom_ref[...]) + _dot(ya_ref[...], woa_ref[...])
    x1_ref[...] = x1
    ms = jnp.mean(x1 * x1, axis=-1, keepdims=True)
    h2f = x1 * lax.rsqrt(ms + RMS_EPS) * g_ref[...]
    h2 = h2f.astype(BF16)
    h2t_ref[...] = pltpu.bitcast(h2f.T.astype(BF16), jnp.uint32)
    qry = _dot(h2, wq_ref[...]).astype(BF16)
    half = P_DK // 2
    for h in range(P_H):
        s1_ref[h] = _dot_nt(k1_ref[...], qry[:, h * P_DK:h * P_DK + half])
        s2_ref[h] = _dot_nt(k2_ref[...], qry[:, h * P_DK + half:(h + 1) * P_DK])


def _out_proj(x2, ym, ya, wo_m, wo_a, g, wq, k1, k2, tm):
    T = x2.shape[0]
    full = lambda a: pl.BlockSpec(a.shape, lambda i: (0, 0))
    return pl.pallas_call(
        _outproj_kernel,
        grid=(T // tm,),
        in_specs=[pl.BlockSpec((tm, D_MODEL), lambda i: (i, 0)),
                  pl.BlockSpec((tm, ym.shape[1]), lambda i: (i, 0)),
                  pl.BlockSpec((tm, ya.shape[1]), lambda i: (i, 0)),
                  full(wo_m), full(wo_a), full(g), full(wq), full(k1), full(k2)],
        out_specs=[pl.BlockSpec((tm, D_MODEL), lambda i: (i, 0)),
                   pl.BlockSpec((D_MODEL // 2, tm), lambda i: (0, i)),
                   pl.BlockSpec((P_H, P_NK, tm), lambda i: (0, 0, i)),
                   pl.BlockSpec((P_H, P_NK, tm), lambda i: (0, 0, i))],
        out_shape=[jax.ShapeDtypeStruct((T, D_MODEL), F32),
                   jax.ShapeDtypeStruct((D_MODEL // 2, T), jnp.uint32),
                   jax.ShapeDtypeStruct((P_H, P_NK, T), F32),
                   jax.ShapeDtypeStruct((P_H, P_NK, T), F32)],
        compiler_params=_cparams(("parallel",)),
        name="out_peerq",
    )(x2, ym, ya, wo_m, wo_a, g, wq, k1, k2)


def _top16(x):
    vals = []
    for _ in range(P_TOPK):
        m = jnp.max(x, axis=0, keepdims=True)
        vals.append(m)
        x = jnp.where(x == m, NEG_INF, x)
    return vals


def _oddeven_merge(lo, hi, r):
    step = r * 2
    if step < hi - lo:
        yield from _oddeven_merge(lo, hi, step)
        yield from _oddeven_merge(lo + r, hi, step)
        yield from [(i, i + r) for i in range(lo + r, hi - r, step)]
    else:
        yield (lo, lo + r)


def _oddeven_merge_sort(lo, hi):
    if hi - lo >= 1:
        mid = lo + (hi - lo) // 2
        yield from _oddeven_merge_sort(lo, mid)
        yield from _oddeven_merge_sort(mid + 1, hi)
        yield from _oddeven_merge(lo, hi, 1)


_SORT16 = tuple(_oddeven_merge_sort(0, P_TOPK - 1))
_SUBLANES = 8


def _cmpx(v, i, j):
    a, b = v[i], v[j]
    v[i] = jnp.maximum(a, b)
    v[j] = jnp.minimum(a, b)


def _top16_sorted(x):
    v = [x[i * _SUBLANES:(i + 1) * _SUBLANES, :] for i in range(P_TOPK)]
    for i, j in _SORT16:
        _cmpx(v, i, j)
    for shift in (4, 2, 1):
        o = [pltpu.roll(t, shift, axis=0) for t in v]
        v = [jnp.maximum(v[i], o[P_TOPK - 1 - i]) for i in range(P_TOPK)]
        for d in (8, 4, 2, 1):
            for i in range(P_TOPK):
                if not i & d:
                    _cmpx(v, i, i + d)
    return v


_CAND = [(a, b) for a in range(P_TOPK) for b in range(P_TOPK) if (a + 1) * (b + 1) <= P_TOPK]


def _route_kernel(s1_ref, s2_ref, r2_ref, e2_ref, cnt_ref, f_ref):
    s1 = s1_ref[0]
    s2 = s2_ref[0]
    ng = P_NK // _SUBLANES
    v1s = _top16_sorted(s1)
    v2s = _top16_sorted(s2)
    v1 = [t[0:1, :] for t in v1s]
    v2 = [t[0:1, :] for t in v2s]
    cand = jnp.concatenate([v1[a] + v2[b] for a, b in _CAND], axis=0)
    tau = _top16(cand)[P_TOPK - 1]
    top = v1[0] + v2[0]
    z = jnp.zeros_like(tau)
    cnt_a = [jnp.zeros_like(tau) for _ in range(P_TOPK)]
    for a, b in _CAND:
        sm = v1[a] + v2[b]
        sel = sm >= tau
        z = z + jnp.where(sel, jnp.exp(sm - top), 0.0)
        cnt_a[a] = cnt_a[a] + jnp.where(sel, 1.0, 0.0)
    cnt_b = [jnp.broadcast_to(c, (_SUBLANES, c.shape[1])) for c in cnt_a]
    for i in range(ng):
        rows = slice(i * _SUBLANES, (i + 1) * _SUBLANES)
        x1, x2 = s1[rows, :], s2[rows, :]
        r2 = jnp.full_like(x2, float(P_TOPK))
        cnt = jnp.zeros_like(x1)
        for a in reversed(range(P_TOPK)):
            r2 = jnp.where(x2 >= v2s[a], float(a), r2)
            cnt = jnp.where(x1 == v1s[a], cnt_b[a], cnt)
        r2_ref[0, rows, :] = r2
        cnt_ref[0, rows, :] = cnt
    e2_ref[0] = jnp.exp(s2 - v2[0])
    f_ref[0] = jnp.exp(s1 - v1[0]) / z


def _peer_route(s1t, s2t, tl):
    T = s1t.shape[2]
    spec = pl.BlockSpec((1, P_NK, tl), lambda h, i: (h, 0, i))
    shp32 = jax.ShapeDtypeStruct((P_H, P_NK, T), F32)
    return pl.pallas_call(
        _route_kernel,
        grid=(P_H, T // tl),
        in_specs=[spec, spec],
        out_specs=[spec, spec, spec, spec],
        out_shape=[shp32, shp32, shp32, shp32],
        compiler_params=_cparams(("parallel", "parallel")),
        name="peer_route",
    )(s1t, s2t)


def _gelu2(x):
    return x * (1.0 + lax.erf(x * (2.0 ** -0.5)))


_LANES = 128
_PACK = 16
_N1_GROUP = 4


def _peer_kernel(h2t_ref, x1_ref, u_ref, vt_ref, r2f_ref, e2f_ref, cnt_ref, f_ref, o_ref,
                 acc_ref, a_ref, p_ref, r2_ref, e2_ref, *, nb, tm):
    e = pl.program_id(1)

    @pl.when(e == 0)
    def _():
        acc_ref[...] = jnp.zeros_like(acc_ref)
        for h in range(P_H):
            r2_ref[h] = r2f_ref[h].astype(BF16)
            e2_ref[h] = e2f_ref[h].astype(BF16)

    a_ref[...] = _dot(u_ref[...], pltpu.bitcast(h2t_ref[...], BF16))
    n1_0 = pl.multiple_of(e * nb, nb)
    zero = jnp.zeros((_PACK, _LANES), BF16)
    for g in range(tm // _LANES):
        lanes = slice(g * _LANES, (g + 1) * _LANES)
        for jc in range(nb // _N1_GROUP):
            w = [[None] * (P_NK // _PACK) for _ in range(_N1_GROUP)]
            for h in range(P_H):
                cnt8 = cnt_ref[h, pl.ds(n1_0, nb), lanes]
                f8 = 0.5 * f_ref[h, pl.ds(n1_0, nb), lanes]
                r2t = [r2_ref[h, r * _PACK:(r + 1) * _PACK, lanes] for r in range(P_NK // _PACK)]
                e2t = [e2_ref[h, r * _PACK:(r + 1) * _PACK, lanes] for r in range(P_NK // _PACK)]
                for jj in range(_N1_GROUP):
                    j = jc * _N1_GROUP + jj
                    cb = jnp.broadcast_to(cnt8[j:j + 1, :], (_PACK, _LANES)).astype(BF16)
                    fb = jnp.broadcast_to(f8[j:j + 1, :], (_PACK, _LANES)).astype(BF16)
                    for r in range(P_NK // _PACK):
                        t = jnp.where(r2t[r] < cb, e2t[r], zero) * fb
                        w[jj][r] = t if w[jj][r] is None else w[jj][r] + t
            for jj in range(_N1_GROUP):
                j = jc * _N1_GROUP + jj
                for r in range(P_NK // _PACK):
                    rows = slice(j * P_NK + r * _PACK, j * P_NK + (r + 1) * _PACK)
                    p_ref[rows, lanes] = w[jj][r] * _gelu2(a_ref[rows, lanes]).astype(BF16)
    acc_ref[...] += _dot(vt_ref[...], p_ref[...])

    @pl.when(e == pl.num_programs(1) - 1)
    def _():
        o_ref[...] = x1_ref[...] + acc_ref[...].T


def _peer_mlp(h2t, x1, u, vt, r2t, e2t, cntt, ft, tm, nb):
    T = x1.shape[0]
    ne = P_NK // nb
    te = nb * P_NK
    rspec = pl.BlockSpec((P_H, P_NK, tm), lambda i, e: (0, 0, i))
    return pl.pallas_call(
        functools.partial(_peer_kernel, nb=nb, tm=tm),
        grid=(T // tm, ne),
        in_specs=[pl.BlockSpec((D_MODEL // 2, tm), lambda i, e: (0, i)),
                  pl.BlockSpec((tm, D_MODEL), lambda i, e: (i, 0)),
                  pl.BlockSpec((te, D_MODEL), lambda i, e: (e, 0)),
                  pl.BlockSpec((D_MODEL, te), lambda i, e: (0, e)),
                  rspec, rspec, rspec, rspec],
        out_specs=pl.BlockSpec((tm, D_MODEL), lambda i, e: (i, 0)),
        out_shape=jax.ShapeDtypeStruct((T, D_MODEL), F32),
        scratch_shapes=[pltpu.VMEM((D_MODEL, tm), F32),
                        pltpu.VMEM((te, tm), F32),
                        pltpu.VMEM((te, tm), BF16),
                        pltpu.VMEM((P_H, P_NK, tm), BF16),
                        pltpu.VMEM((P_H, P_NK, tm), BF16)],
        compiler_params=_cparams(("parallel", "arbitrary")),
        name="peer_mlp",
    )(h2t, x1, u, vt, r2t, e2t, cntt, ft)


def _swap_halves(t, axis=-1):
    a, b = jnp.split(t, 2, axis=axis)
    return jnp.concatenate([b, a], axis=axis)


def _pad_cols(t, n):
    return jnp.pad(t, ((0, 0), (0, n - t.shape[1])))


def _layer(x2, pos_col, B, S, attn_norm_g, w_in, b_igate, b_fgate, mlstm_norm_g, cq_norm_g, w_uq,
           ckv_norm_g, w_ukv, q_norm_g, k_norm_g, w_out, ffn_norm_g, peer_w_q, peer_keys_1,
           peer_keys_2, peer_u, peer_v):
    T = B * S
    n_qkvo = 4 * M_H * M_DK
    o_gate = n_qkvo
    o_cq = o_gate + 2 * M_H
    o_ckv = o_cq + A_QR
    o_kr = o_ckv + A_KVR

    w_main = w_in[:, :n_qkvo].astype(BF16)
    w_kr = w_in[:, o_kr:o_kr + A_ROPE]
    w_lat = jnp.concatenate([w_in[:, o_cq:o_kr], _pad_cols(w_kr, 128),
                             _pad_cols(_swap_halves(w_kr), 128)], axis=1).astype(BF16)
    w_gate = _pad_cols(w_in[:, o_gate:o_gate + 2 * M_H], 128).astype(BF16)
    bias_row = _pad_cols(jnp.concatenate([b_igate, b_fgate])[None, :], 128)

    wq_h = w_uq.reshape(A_QR, A_H, A_DQK)
    wq_pad = jnp.pad(wq_h, ((0, 0), (0, 0), (0, A_DPAD - A_DQK))).reshape(A_QR, A_H * A_DPAD).astype(BF16)
    wq_sw = jnp.pad(_swap_halves(wq_h[:, :, A_NOPE:]), ((0, 0), (0, 0), (0, 128 - A_ROPE)))
    wq_sw = wq_sw.reshape(A_QR, A_H * 128).astype(BF16)
    w_kv = w_ukv.astype(BF16)
    qg = _pad_cols(q_norm_g[None, :], A_DPAD)
    qgs = _pad_cols(_swap_halves(q_norm_g[A_NOPE:])[None, :], 128)
    kg = _pad_cols(k_norm_g[None, :], A_DPAD)
    kgs = _pad_cols(_swap_halves(k_norm_g[A_NOPE:])[None, :], 128)
    inv_freq = ROPE_THETA ** (-jnp.arange(0, A_ROPE, 2, dtype=F32) / A_ROPE)
    freq_row = _pad_cols(jnp.concatenate([inv_freq, inv_freq])[None, :], 128)
    half = A_ROPE // 2
    sign_row = _pad_cols(jnp.concatenate([-jnp.ones((half,), F32), jnp.ones((half,), F32)])[None, :], 128)

    qkvo, lat, gates = _in_proj(x2, attn_norm_g[None, :], w_main, w_lat, w_gate, tm=min(512, T))
    y_m = _mlstm(qkvo, gates, bias_row, mlstm_norm_g[None, :], B, S, L=min(256, S), nbatch=1)
    consts = [cq_norm_g[None, :], ckv_norm_g[None, :], wq_pad, wq_sw, w_kv, qg, qgs, kg, kgs,
              freq_row, sign_row]
    q, k, v = _mla_prep(lat, pos_col, consts, tm=min(512, T))
    y_a = _mla_attn(q, k, v, B, S, tq=min(512, S), hp=A_H)
    wo = w_out.astype(BF16)
    x1, h2t, s1t, s2t = _out_proj(x2, y_m, y_a, wo[:M_H * M_DV], wo[M_H * M_DV:], ffn_norm_g[None, :],
                                  peer_w_q.astype(BF16), peer_keys_1.astype(BF16),
                                  peer_keys_2.astype(BF16), tm=min(512, T))
    r2t, e2t, cntt, ft = _peer_route(s1t, s2t, tl=min(256, T))
    return _peer_mlp(h2t, x1, peer_u.astype(BF16), peer_v.astype(BF16).T, r2t, e2t, cntt, ft,
                     tm=min(512, T), nb=8)


def kernel(x, positions, attn_norm_g, w_in, b_igate, b_fgate, mlstm_norm_g, cq_norm_g, w_uq, ckv_norm_g,
           w_ukv, q_norm_g, k_norm_g, w_out, ffn_norm_g, peer_w_q, peer_keys_1, peer_keys_2, peer_u, peer_v):
    B, S, D = x.shape
    x2 = x.reshape(B * S, D)
    pos_col = positions.reshape(B * S, 1).astype(F32)
    for l in range(attn_norm_g.shape[0]):
        x2 = _layer(x2, pos_col, B, S, attn_norm_g[l], w_in[l], b_igate[l], b_fgate[l], mlstm_norm_g[l],
                    cq_norm_g[l], w_uq[l], ckv_norm_g[l], w_ukv[l], q_norm_g[l], k_norm_g[l], w_out[l],
                    ffn_norm_g[l], peer_w_q[l], peer_keys_1[l], peer_keys_2[l], peer_u[l], peer_v[l])
    return x2.reshape(B, S, D)
```

```python
import functools

import jax
import jax.numpy as jnp
from jax import lax
from jax.experimental import pallas as pl
from jax.experimental.pallas import tpu as pltpu

F32 = jnp.float32
BF16 = jnp.bfloat16

D_MODEL = 1024
RMS_EPS = 1e-6
M_H, M_DK, M_DV = 4, 128, 128
A_H, A_QR, A_KVR, A_NOPE, A_ROPE, A_DV = 4, 256, 128, 128, 64, 128
A_DQK = A_NOPE + A_ROPE
A_DPAD = 256
ROPE_THETA = 10000.0
P_H, P_NK, P_DK, P_TOPK = 8, 128, 256, 16
NEG_INF = float("-inf")

VMEM_LIMIT = 56 * 1024 * 1024


def _cparams(sem):
    return pltpu.CompilerParams(dimension_semantics=sem, vmem_limit_bytes=VMEM_LIMIT)


def _dot(a, b):
    return jnp.dot(a, b, preferred_element_type=F32)


def _dot_nt(a, b):
    return lax.dot_general(a, b, (((1,), (1,)), ((), ())), preferred_element_type=F32)


def _dot_tn(a, b):
    return lax.dot_general(a, b, (((0,), (0,)), ((), ())), preferred_element_type=F32)


def _inproj_kernel(x_ref, g_ref, wm_ref, wl_ref, wg_ref, qkvo_ref, lat_ref, gate_ref):
    x = x_ref[...]
    ms = jnp.mean(x * x, axis=-1, keepdims=True)
    h = (x * lax.rsqrt(ms + RMS_EPS) * g_ref[...]).astype(BF16)
    main = _dot(h, wm_ref[...])
    kcol = lax.broadcasted_iota(jnp.int32, (1, main.shape[1]), 1) // (M_H * M_DK) == 1
    main = jnp.where(kcol, main * (M_DK ** -0.5), main)
    qkvo_ref[...] = main.astype(BF16)
    lat_ref[...] = _dot(h, wl_ref[...])
    gate_ref[...] = _dot(h, wg_ref[...])


def _in_proj(x2, g, w_main, w_lat, w_gate, tm):
    T = x2.shape[0]
    n_main, n_lat, n_gate = w_main.shape[1], w_lat.shape[1], w_gate.shape[1]
    full = lambda shape: pl.BlockSpec(shape, lambda i: (0, 0))
    return pl.pallas_call(
        _inproj_kernel,
        grid=(T // tm,),
        in_specs=[pl.BlockSpec((tm, D_MODEL), lambda i: (i, 0)), full((1, D_MODEL)),
                  full(w_main.shape), full(w_lat.shape), full(w_gate.shape)],
        out_specs=[pl.BlockSpec((tm, n_main), lambda i: (i, 0)),
                   pl.BlockSpec((tm, n_lat), lambda i: (i, 0)),
                   pl.BlockSpec((tm, n_gate), lambda i: (i, 0))],
        out_shape=[jax.ShapeDtypeStruct((T, n_main), BF16),
                   jax.ShapeDtypeStruct((T, n_lat), F32),
                   jax.ShapeDtypeStruct((T, n_gate), F32)],
        compiler_params=_cparams(("parallel",)),
        name="in_proj",
    )(x2, g, w_main, w_lat, w_gate)


def _log_sigmoid(x):
    return -(jnp.maximum(-x, 0.0) + jnp.log1p(jnp.exp(-jnp.abs(x))))


def _mlstm_kernel(qkvo_ref, gate_ref, bias_ref, ng_ref, y_ref, c_ref, m_ref, *, L, nbatch):
    c_idx = pl.program_id(1)

    @pl.when(c_idx == 0)
    def _():
        c_ref[...] = jnp.zeros_like(c_ref)
        m_ref[...] = jnp.zeros_like(m_ref)

    H, DK, DV = M_H, M_DK, M_DV
    row = lax.broadcasted_iota(jnp.int32, (L, L), 0)
    col = lax.broadcasted_iota(jnp.int32, (L, L), 1)
    causal = col <= row
    tri = jnp.where(causal, 1.0, 0.0)
    lane = lax.broadcasted_iota(jnp.int32, (L, 128), 1)
    ones_col = jnp.where(lane == 0, 1.0, 0.0).astype(BF16)

    for bb in range(nbatch):
        G = gate_ref[bb] + bias_ref[...]
        LF = _log_sigmoid(G)
        Bc = jnp.dot(tri, LF, preferred_element_type=F32, precision=lax.Precision.HIGHEST)
        ZT = jnp.where(lane < H, G, Bc).T

        for h in range(H):
            q = qkvo_ref[bb, :, h * DK:(h + 1) * DK]
            k = qkvo_ref[bb, :, H * DK + h * DK: H * DK + (h + 1) * DK]
            v = qkvo_ref[bb, :, 2 * H * DK + h * DV: 2 * H * DK + (h + 1) * DV]
            o = qkvo_ref[bb, :, 3 * H * DK + h * DV: 3 * H * DK + (h + 1) * DV]
            i_col = G[:, h:h + 1]
            b_col = Bc[:, H + h:H + h + 1]
            i_row = ZT[h:h + 1, :]
            b_row = ZT[H + h:H + h + 1, :]
            m_prev = m_ref[bb * H + h]
            c_prev = c_ref[bb * H + h]

            a_col = b_col + m_prev
            D = jnp.where(causal, b_col - b_row + i_row, NEG_INF)
            m_t = jnp.maximum(a_col, jnp.max(D, axis=-1, keepdims=True))
            Dw = jnp.exp(D - m_t)
            aw = jnp.exp(a_col - m_t)
            s = (_dot_nt(q, k) * Dw).astype(BF16)
            v_aug = jnp.concatenate([v, ones_col], axis=-1)
            num = aw * _dot(q, c_prev.astype(BF16)) + _dot(s, v_aug)
            den = num[:, DV:DV + 1]
            hval = num[:, :DV] / jnp.maximum(jnp.abs(den), jnp.exp(-m_t))

            bL = b_col[L - 1:L, :]
            g_col = bL - b_col + i_col
            m_new = jnp.maximum(bL + m_prev, jnp.max(g_col, axis=0, keepdims=True))
            decay = jnp.exp(bL + m_prev - m_new)
            w_col = jnp.exp(g_col - m_new)
            wv = (w_col * v_aug.astype(F32)).astype(BF16)
            c_ref[bb * H + h] = decay * c_prev + _dot_tn(k, wv)
            m_ref[bb * H + h] = m_new

            ms = jnp.mean(hval * hval, axis=-1, keepdims=True)
            hn = hval * lax.rsqrt(ms + RMS_EPS) * ng_ref[:, h * DV:(h + 1) * DV]
            y_ref[bb, :, h * DV:(h + 1) * DV] = (jax.nn.sigmoid(o.astype(F32)) * hn).astype(BF16)


def _mlstm(qkvo, gates, bias_row, norm_g, B, S, L, nbatch):
    nc = S // L
    n_in = qkvo.shape[1]
    y = pl.pallas_call(
        functools.partial(_mlstm_kernel, L=L, nbatch=nbatch),
        grid=(B // nbatch, nc),
        in_specs=[pl.BlockSpec((nbatch, L, n_in), lambda b, c: (b, c, 0)),
                  pl.BlockSpec((nbatch, L, 128), lambda b, c: (b, c, 0)),
                  pl.BlockSpec((1, 128), lambda b, c: (0, 0)),
                  pl.BlockSpec((1, M_H * M_DV), lambda b, c: (0, 0))],
        out_specs=pl.BlockSpec((nbatch, L, M_H * M_DV), lambda b, c: (b, c, 0)),
        out_shape=jax.ShapeDtypeStruct((B, S, M_H * M_DV), BF16),
        scratch_shapes=[pltpu.VMEM((nbatch * M_H, M_DK, 2 * M_DV), F32),
                        pltpu.VMEM((nbatch * M_H, 1, 1), F32)],
        compiler_params=_cparams(("parallel", "arbitrary")),
        name="mlstm",
    )(qkvo.reshape(B, S, n_in), gates.reshape(B, S, 128), bias_row, norm_g)
    return y.reshape(B * S, M_H * M_DV)


def _mla_prep_kernel(lat_ref, pos_ref, cqg_ref, ckvg_ref, wq_ref, wqs_ref, wkv_ref,
                     qg_ref, qgs_ref, kg_ref, kgs_ref, freq_ref, sign_ref,
                     q_ref, k_ref, v_ref):
    H = A_H
    cq = lat_ref[:, :A_QR]
    ckv = lat_ref[:, A_QR:A_QR + A_KVR]
    kr = lat_ref[:, A_QR + A_KVR:A_QR + A_KVR + 128]
    krs = lat_ref[:, A_QR + A_KVR + 128:A_QR + A_KVR + 256]

    def rms(t, g):
        return t * lax.rsqrt(jnp.mean(t * t, axis=-1, keepdims=True) + RMS_EPS) * g

    cqn = rms(cq, cqg_ref[...]).astype(BF16)
    ckvn = rms(ckv, ckvg_ref[...]).astype(BF16)
    qf = _dot(cqn, wq_ref[...])
    qs = _dot(cqn, wqs_ref[...])
    kv = _dot(ckvn, wkv_ref[...])

    ang = pos_ref[...] * freq_ref[...]
    cos = jnp.cos(ang)
    sin = jnp.sin(ang) * sign_ref[...]
    scale = A_DQK ** -0.5
    kr_ss = jnp.sum(kr * kr, axis=-1, keepdims=True)

    for h in range(H):
        qh = qf[:, h * A_DPAD:(h + 1) * A_DPAD]
        rstd = lax.rsqrt(jnp.sum(qh * qh, axis=-1, keepdims=True) / A_DQK + RMS_EPS) * scale
        q_nope = qh[:, :128] * rstd * qg_ref[:, :128]
        q_rope = qh[:, 128:] * rstd * qg_ref[:, 128:]
        q_rope_s = qs[:, h * 128:(h + 1) * 128] * rstd * qgs_ref[...]
        q_ref[:, h * A_DPAD:h * A_DPAD + 128] = q_nope.astype(BF16)
        q_ref[:, h * A_DPAD + 128:(h + 1) * A_DPAD] = (q_rope * cos + q_rope_s * sin).astype(BF16)

        kn = kv[:, h * 256:h * 256 + 128]
        rstd_k = lax.rsqrt((jnp.sum(kn * kn, axis=-1, keepdims=True) + kr_ss) / A_DQK + RMS_EPS)
        k_nope = kn * rstd_k * kg_ref[:, :128]
        k_rope = kr * rstd_k * kg_ref[:, 128:]
        k_rope_s = krs * rstd_k * kgs_ref[...]
        k_ref[:, h * A_DPAD:h * A_DPAD + 128] = k_nope.astype(BF16)
        k_ref[:, h * A_DPAD + 128:(h + 1) * A_DPAD] = (k_rope * cos + k_rope_s * sin).astype(BF16)
        v_ref[:, h * A_DV:(h + 1) * A_DV] = kv[:, h * 256 + 128:(h + 1) * 256].astype(BF16)


def _mla_prep(lat, pos_col, consts, tm):
    T = lat.shape[0]
    full = lambda a: pl.BlockSpec(a.shape, lambda i: (0, 0))
    return pl.pallas_call(
        _mla_prep_kernel,
        grid=(T // tm,),
        in_specs=[pl.BlockSpec((tm, lat.shape[1]), lambda i: (i, 0)),
                  pl.BlockSpec((tm, 1), lambda i: (i, 0))] + [full(c) for c in consts],
        out_specs=[pl.BlockSpec((tm, A_H * A_DPAD), lambda i: (i, 0)),
                   pl.BlockSpec((tm, A_H * A_DPAD), lambda i: (i, 0)),
                   pl.BlockSpec((tm, A_H * A_DV), lambda i: (i, 0))],
        out_shape=[jax.ShapeDtypeStruct((T, A_H * A_DPAD), BF16),
                   jax.ShapeDtypeStruct((T, A_H * A_DPAD), BF16),
                   jax.ShapeDtypeStruct((T, A_H * A_DV), BF16)],
        compiler_params=_cparams(("parallel",)),
        name="mla_prep",
    )(lat, pos_col, *consts)


def _attn_kernel(q_ref, k_ref, v_ref, o_ref, *, tq, hp):
    qi = pl.program_id(2)

    def chunk(j, carry, masked):
        rows = pl.ds(pl.multiple_of(j * tq, tq), tq)
        out = []
        for h in range(hp):
            m, l, acc = carry[h]
            q = q_ref[:, h * A_DPAD:(h + 1) * A_DPAD]
            k = k_ref[rows, h * A_DPAD:(h + 1) * A_DPAD]
            v = v_ref[rows, h * A_DV:(h + 1) * A_DV]
            s = _dot_nt(q, k)
            if masked:
                row = lax.broadcasted_iota(jnp.int32, (tq, tq), 0)
                col = lax.broadcasted_iota(jnp.int32, (tq, tq), 1)
                s = jnp.where(col <= row, s, NEG_INF)
            m_new = jnp.maximum(m, jnp.max(s, axis=-1, keepdims=True))
            alpha = jnp.exp(m - m_new)
            p = jnp.exp(s - m_new)
            l = alpha * l + jnp.sum(p, axis=-1, keepdims=True)
            acc = alpha * acc + _dot(p.astype(BF16), v)
            out.append((m_new, l, acc))
        return tuple(out)

    init = tuple((jnp.full((tq, 1), NEG_INF, F32), jnp.zeros((tq, 1), F32), jnp.zeros((tq, A_DV), F32))
                 for _ in range(hp))
    carry = lax.fori_loop(0, qi, lambda j, c: chunk(j, c, False), init)
    carry = chunk(qi, carry, True)
    for h in range(hp):
        m, l, acc = carry[h]
        o_ref[:, h * A_DV:(h + 1) * A_DV] = (acc / l).astype(BF16)


def _mla_attn(q, k, v, B, S, tq, hp):
    T = B * S
    nq = S // tq
    return pl.pallas_call(
        functools.partial(_attn_kernel, tq=tq, hp=hp),
        grid=(B, A_H // hp, nq),
        in_specs=[pl.BlockSpec((tq, hp * A_DPAD), lambda b, h, i: (b * nq + i, h)),
                  pl.BlockSpec((S, hp * A_DPAD), lambda b, h, i: (b, h)),
                  pl.BlockSpec((S, hp * A_DV), lambda b, h, i: (b, h))],
        out_specs=pl.BlockSpec((tq, hp * A_DV), lambda b, h, i: (b * nq + i, h)),
        out_shape=jax.ShapeDtypeStruct((T, A_H * A_DV), BF16),
        compiler_params=_cparams(("parallel", "parallel", "arbitrary")),
        name="mla_attn",
    )(q, k, v)


def _outproj_kernel(x_ref, ym_ref, ya_ref, wom_ref, woa_ref, g_ref, wq_ref, k1_ref, k2_ref,
                    x1_ref, h2t_ref, s1_ref, s2_ref):
    x1 = x_ref[...] + _dot(ym_ref[...], wom_ref[...]) + _dot(ya_ref[...], woa_ref[...])
    x1_ref[...] = x1
    ms = jnp.mean(x1 * x1, axis=-1, keepdims=True)
    h2f = x1 * lax.rsqrt(ms + RMS_EPS) * g_ref[...]
    h2 = h2f.astype(BF16)
    h2t_ref[...] = pltpu.bitcast(h2f.T.astype(BF16), jnp.uint32)
    qry = _dot(h2, wq_ref[...]).astype(BF16)
    half = P_DK // 2
    for h in range(P_H):
        s1_ref[h] = _dot_nt(k1_ref[...], qry[:, h * P_DK:h * P_DK + half])
        s2_ref[h] = _dot_nt(k2_ref[...], qry[:, h * P_DK + half:(h + 1) * P_DK])


def _out_proj(x2, ym, ya, wo_m, wo_a, g, wq, k1, k2, tm):
    T = x2.shape[0]
    full = lambda a: pl.BlockSpec(a.shape, lambda i: (0, 0))
    return pl.pallas_call(
        _outproj_kernel,
        grid=(T // tm,),
        in_specs=[pl.BlockSpec((tm, D_MODEL), lambda i: (i, 0)),
                  pl.BlockSpec((tm, ym.shape[1]), lambda i: (i, 0)),
                  pl.BlockSpec((tm, ya.shape[1]), lambda i: (i, 0)),
                  full(wo_m), full(wo_a), full(g), full(wq), full(k1), full(k2)],
        out_specs=[pl.BlockSpec((tm, D_MODEL), lambda i: (i, 0)),
                   pl.BlockSpec((D_MODEL // 2, tm), lambda i: (0, i)),
                   pl.BlockSpec((P_H, P_NK, tm), lambda i: (0, 0, i)),
                   pl.BlockSpec((P_H, P_NK, tm), lambda i: (0, 0, i))],
        out_shape=[jax.ShapeDtypeStruct((T, D_MODEL), F32),
                   jax.ShapeDtypeStruct((D_MODEL // 2, T), jnp.uint32),
                   jax.ShapeDtypeStruct((P_H, P_NK, T), F32),
                   jax.ShapeDtypeStruct((P_H, P_NK, T), F32)],
        compiler_params=_cparams(("parallel",)),
        name="out_peerq",
    )(x2, ym, ya, wo_m, wo_a, g, wq, k1, k2)


def _top16(x):
    vals = []
    for _ in range(P_TOPK):
        m = jnp.max(x, axis=0, keepdims=True)
        vals.append(m)
        x = jnp.where(x == m, NEG_INF, x)
    return vals


def _oddeven_merge(lo, hi, r):
    step = r * 2
    if step < hi - lo:
        yield from _oddeven_merge(lo, hi, step)
        yield from _oddeven_merge(lo + r, hi, step)
        yield from [(i, i + r) for i in range(lo + r, hi - r, step)]
    else:
        yield (lo, lo + r)


def _oddeven_merge_sort(lo, hi):
    if hi - lo >= 1:
        mid = lo + (hi - lo) // 2
        yield from _oddeven_merge_sort(lo, mid)
        yield from _oddeven_merge_sort(mid + 1, hi)
        yield from _oddeven_merge(lo, hi, 1)


_SORT16 = tuple(_oddeven_merge_sort(0, P_TOPK - 1))
_SUBLANES = 8


def _cmpx(v, i, j):
    a, b = v[i], v[j]
    v[i] = jnp.maximum(a, b)
    v[j] = jnp.minimum(a, b)


def _top16_sorted(x):
    v = [x[i * _SUBLANES:(i + 1) * _SUBLANES, :] for i in range(P_TOPK)]
    for i, j in _SORT16:
        _cmpx(v, i, j)
    for shift in (4, 2, 1):
        o = [pltpu.roll(t, shift, axis=0) for t in v]
        v = [jnp.maximum(v[i], o[P_TOPK - 1 - i]) for i in range(P_TOPK)]
        for d in (8, 4, 2, 1):
            for i in range(P_TOPK):
                if not i & d:
                    _cmpx(v, i, i + d)
    return v


_CAND = [(a, b) for a in range(P_TOPK) for b in range(P_TOPK) if (a + 1) * (b + 1) <= P_TOPK]


def _route_kernel(s1_ref, s2_ref, r2_ref, e2_ref, cnt_ref, f_ref):
    s1 = s1_ref[0]
    s2 = s2_ref[0]
    ng = P_NK // _SUBLANES
    v1s = _top16_sorted(s1)
    v2s = _top16_sorted(s2)
    v1 = [t[0:1, :] for t in v1s]
    v2 = [t[0:1, :] for t in v2s]
    cand = jnp.concatenate([v1[a] + v2[b] for a, b in _CAND], axis=0)
    tau = _top16(cand)[P_TOPK - 1]
    top = v1[0] + v2[0]
    z = jnp.zeros_like(tau)
    cnt_a = [jnp.zeros_like(tau) for _ in range(P_TOPK)]
    for a, b in _CAND:
        sm = v1[a] + v2[b]
        sel = sm >= tau
        z = z + jnp.where(sel, jnp.exp(sm - top), 0.0)
        cnt_a[a] = cnt_a[a] + jnp.where(sel, 1.0, 0.0)
    cnt_b = [jnp.broadcast_to(c, (_SUBLANES, c.shape[1])) for c in cnt_a]
    for i in range(ng):
        rows = slice(i * _SUBLANES, (i + 1) * _SUBLANES)
        x1, x2 = s1[rows, :], s2[rows, :]
        r2 = jnp.full_like(x2, float(P_TOPK))
        cnt = jnp.zeros_like(x1)
        for a in reversed(range(P_TOPK)):
            r2 = jnp.where(x2 >= v2s[a], float(a), r2)
            cnt = jnp.where(x1 == v1s[a], cnt_b[a], cnt)
        r2_ref[0, rows, :] = r2
        cnt_ref[0, rows, :] = cnt
    e2_ref[0] = jnp.exp(s2 - v2[0])
    f_ref[0] = jnp.exp(s1 - v1[0]) / z


def _peer_route(s1t, s2t, tl):
    T = s1t.shape[2]
    spec = pl.BlockSpec((1, P_NK, tl), lambda h, i: (h, 0, i))
    shp32 = jax.ShapeDtypeStruct((P_H, P_NK, T), F32)
    return pl.pallas_call(
        _route_kernel,
        grid=(P_H, T // tl),
        in_specs=[spec, spec],
        out_specs=[spec, spec, spec, spec],
        out_shape=[shp32, shp32, shp32, shp32],
        compiler_params=_cparams(("parallel", "parallel")),
        name="peer_route",
    )(s1t, s2t)


def _gelu2(x):
    return x * (1.0 + lax.erf(x * (2.0 ** -0.5)))


_LANES = 128
_PACK = 16
_N1_GROUP = 4


_U_SPLIT = 4
_V_SPLIT = 2


def _peer_kernel(h2t_ref, x1_ref, *refs, nb, tm):
    u_refs = refs[:_U_SPLIT]
    vt_refs = refs[_U_SPLIT:_U_SPLIT + _V_SPLIT]
    (r2f_ref, e2f_ref, cnt_ref, f_ref, o_ref,
     acc_ref, a_ref, p_ref, r2_ref, e2_ref) = refs[_U_SPLIT + _V_SPLIT:]
    e = pl.program_id(1)

    @pl.when(e == 0)
    def _():
        acc_ref[...] = jnp.zeros_like(acc_ref)
        for h in range(P_H):
            r2_ref[h] = r2f_ref[h].astype(BF16)
            e2_ref[h] = e2f_ref[h].astype(BF16)

    te = nb * P_NK
    us = te // _U_SPLIT
    h2t = pltpu.bitcast(h2t_ref[...], BF16)
    for i, u_ref in enumerate(u_refs):
        a_ref[i * us:(i + 1) * us, :] = _dot(u_ref[...], h2t)
    n1_0 = pl.multiple_of(e * nb, nb)
    zero = jnp.zeros((_PACK, _LANES), BF16)
    for g in range(tm // _LANES):
        lanes = slice(g * _LANES, (g + 1) * _LANES)
        for jc in range(nb // _N1_GROUP):
            w = [[None] * (P_NK // _PACK) for _ in range(_N1_GROUP)]
            for h in range(P_H):
                cnt8 = cnt_ref[h, pl.ds(n1_0, nb), lanes]
                f8 = 0.5 * f_ref[h, pl.ds(n1_0, nb), lanes]
                r2t = [r2_ref[h, r * _PACK:(r + 1) * _PACK, lanes] for r in range(P_NK // _PACK)]
                e2t = [e2_ref[h, r * _PACK:(r + 1) * _PACK, lanes] for r in range(P_NK // _PACK)]
                for jj in range(_N1_GROUP):
                    j = jc * _N1_GROUP + jj
                    cb = jnp.broadcast_to(cnt8[j:j + 1, :], (_PACK, _LANES)).astype(BF16)
                    fb = jnp.broadcast_to(f8[j:j + 1, :], (_PACK, _LANES)).astype(BF16)
                    for r in range(P_NK // _PACK):
                        t = jnp.where(r2t[r] < cb, e2t[r], zero) * fb
                        w[jj][r] = t if w[jj][r] is None else w[jj][r] + t
            for jj in range(_N1_GROUP):
                j = jc * _N1_GROUP + jj
                for r in range(P_NK // _PACK):
                    rows = slice(j * P_NK + r * _PACK, j * P_NK + (r + 1) * _PACK)
                    p_ref[rows, lanes] = w[jj][r] * _gelu2(a_ref[rows, lanes]).astype(BF16)
    vs = te // _V_SPLIT
    upd = _dot(vt_refs[0][...], p_ref[0:vs, :])
    for i in range(1, _V_SPLIT):
        upd = upd + _dot(vt_refs[i][...], p_ref[i * vs:(i + 1) * vs, :])
    acc_ref[...] += upd

    @pl.when(e == pl.num_programs(1) - 1)
    def _():
        o_ref[...] = x1_ref[...] + acc_ref[...].T


def _peer_mlp(h2t, x1, u, vt, r2t, e2t, cntt, ft, tm, nb):
    T = x1.shape[0]
    ne = P_NK // nb
    te = nb * P_NK
    rspec = pl.BlockSpec((P_H, P_NK, tm), lambda i, e: (0, 0, i))
    u_specs = [pl.BlockSpec((te // _U_SPLIT, D_MODEL), functools.partial(lambda i, e, k: (e * _U_SPLIT + k, 0), k=k))
               for k in range(_U_SPLIT)]
    v_specs = [pl.BlockSpec((D_MODEL, te // _V_SPLIT), functools.partial(lambda i, e, k: (0, e * _V_SPLIT + k), k=k))
               for k in range(_V_SPLIT)]
    return pl.pallas_call(
        functools.partial(_peer_kernel, nb=nb, tm=tm),
        grid=(T // tm, ne),
        in_specs=[pl.BlockSpec((D_MODEL // 2, tm), lambda i, e: (0, i)),
                  pl.BlockSpec((tm, D_MODEL), lambda i, e: (i, 0))] + u_specs + v_specs +
                 [rspec, rspec, rspec, rspec],
        out_specs=pl.BlockSpec((tm, D_MODEL), lambda i, e: (i, 0)),
        out_shape=jax.ShapeDtypeStruct((T, D_MODEL), F32),
        scratch_shapes=[pltpu.VMEM((D_MODEL, tm), F32),
                        pltpu.VMEM((te, tm), F32),
                        pltpu.VMEM((te, tm), BF16),
                        pltpu.VMEM((P_H, P_NK, tm), BF16),
                        pltpu.VMEM((P_H, P_NK, tm), BF16)],
        compiler_params=_cparams(("parallel", "arbitrary")),
        name="peer_mlp",
    )(h2t, x1, *([u] * _U_SPLIT), *([vt] * _V_SPLIT), r2t, e2t, cntt, ft)


def _swap_halves(t, axis=-1):
    a, b = jnp.split(t, 2, axis=axis)
    return jnp.concatenate([b, a], axis=axis)


def _pad_cols(t, n):
    return jnp.pad(t, ((0, 0), (0, n - t.shape[1])))


def _layer(x2, pos_col, B, S, attn_norm_g, w_in, b_igate, b_fgate, mlstm_norm_g, cq_norm_g, w_uq,
           ckv_norm_g, w_ukv, q_norm_g, k_norm_g, w_out, ffn_norm_g, peer_w_q, peer_keys_1,
           peer_keys_2, peer_u, peer_v):
    T = B * S
    n_qkvo = 4 * M_H * M_DK
    o_gate = n_qkvo
    o_cq = o_gate + 2 * M_H
    o_ckv = o_cq + A_QR
    o_kr = o_ckv + A_KVR

    w_main = w_in[:, :n_qkvo].astype(BF16)
    w_kr = w_in[:, o_kr:o_kr + A_ROPE]
    w_lat = jnp.concatenate([w_in[:, o_cq:o_kr], _pad_cols(w_kr, 128),
                             _pad_cols(_swap_halves(w_kr), 128)], axis=1).astype(BF16)
    w_gate = _pad_cols(w_in[:, o_gate:o_gate + 2 * M_H], 128).astype(BF16)
    bias_row = _pad_cols(jnp.concatenate([b_igate, b_fgate])[None, :], 128)

    wq_h = w_uq.reshape(A_QR, A_H, A_DQK)
    wq_pad = jnp.pad(wq_h, ((0, 0), (0, 0), (0, A_DPAD - A_DQK))).reshape(A_QR, A_H * A_DPAD).astype(BF16)
    wq_sw = jnp.pad(_swap_halves(wq_h[:, :, A_NOPE:]), ((0, 0), (0, 0), (0, 128 - A_ROPE)))
    wq_sw = wq_sw.reshape(A_QR, A_H * 128).astype(BF16)
    w_kv = w_ukv.astype(BF16)
    qg = _pad_cols(q_norm_g[None, :], A_DPAD)
    qgs = _pad_cols(_swap_halves(q_norm_g[A_NOPE:])[None, :], 128)
    kg = _pad_cols(k_norm_g[None, :], A_DPAD)
    kgs = _pad_cols(_swap_halves(k_norm_g[A_NOPE:])[None, :], 128)
    inv_freq = ROPE_THETA ** (-jnp.arange(0, A_ROPE, 2, dtype=F32) / A_ROPE)
    freq_row = _pad_cols(jnp.concatenate([inv_freq, inv_freq])[None, :], 128)
    half = A_ROPE // 2
    sign_row = _pad_cols(jnp.concatenate([-jnp.ones((half,), F32), jnp.ones((half,), F32)])[None, :], 128)

    qkvo, lat, gates = _in_proj(x2, attn_norm_g[None, :], w_main, w_lat, w_gate, tm=min(512, T))
    y_m = _mlstm(qkvo, gates, bias_row, mlstm_norm_g[None, :], B, S, L=min(256, S), nbatch=1)
    consts = [cq_norm_g[None, :], ckv_norm_g[None, :], wq_pad, wq_sw, w_kv, qg, qgs, kg, kgs,
              freq_row, sign_row]
    q, k, v = _mla_prep(lat, pos_col, consts, tm=min(512, T))
    y_a = _mla_attn(q, k, v, B, S, tq=min(512, S), hp=A_H)
    wo = w_out.astype(BF16)
    x1, h2t, s1t, s2t = _out_proj(x2, y_m, y_a, wo[:M_H * M_DV], wo[M_H * M_DV:], ffn_norm_g[None, :],
                                  peer_w_q.astype(BF16), peer_keys_1.astype(BF16),
                                  peer_keys_2.astype(BF16), tm=min(512, T))
    r2t, e2t, cntt, ft = _peer_route(s1t, s2t, tl=min(256, T))
    return _peer_mlp(h2t, x1, peer_u.astype(BF16), peer_v.astype(BF16).T, r2t, e2t, cntt, ft,
                     tm=min(512, T), nb=8)


def kernel(x, positions, attn_norm_g, w_in, b_igate, b_fgate, mlstm_norm_g, cq_norm_g, w_uq, ckv_norm_g,
           w_ukv, q_norm_g, k_norm_g, w_out, ffn_norm_g, peer_w_q, peer_keys_1, peer_keys_2, peer_u, peer_v):
    B, S, D = x.shape
    x2 = x.reshape(B * S, D)
    pos_col = positions.reshape(B * S, 1).astype(F32)
    for l in range(attn_norm_g.shape[0]):
        x2 = _layer(x2, pos_col, B, S, attn_norm_g[l], w_in[l], b_igate[l], b_fgate[l], mlstm_norm_g[l],
                    cq_norm_g[l], w_uq[l], ckv_norm_g[l], w_ukv[l], q_norm_g[l], k_norm_g[l], w_out[l],
                    ffn_norm_g[l], peer_w_q[l], peer_keys_1[l], peer_keys_2[l], peer_u[l], peer_v[l])
    return x2.reshape(B, S, D)
```

```python
import functools

import jax
import jax.numpy as jnp
from jax import lax
from jax.experimental import pallas as pl
from jax.experimental.pallas import tpu as pltpu

F32 = jnp.float32
BF16 = jnp.bfloat16

D_MODEL = 1024
RMS_EPS = 1e-6
M_H, M_DK, M_DV = 4, 128, 128
A_H, A_QR, A_KVR, A_NOPE, A_ROPE, A_DV = 4, 256, 128, 128, 64, 128
A_DQK = A_NOPE + A_ROPE
A_DPAD = 256
ROPE_THETA = 10000.0
P_H, P_NK, P_DK, P_TOPK = 8, 128, 256, 16
NEG_INF = float("-inf")

VMEM_LIMIT = 56 * 1024 * 1024


def _cparams(sem):
    return pltpu.CompilerParams(dimension_semantics=sem, vmem_limit_bytes=VMEM_LIMIT)


def _dot(a, b):
    return jnp.dot(a, b, preferred_element_type=F32)


def _dot_nt(a, b):
    return lax.dot_general(a, b, (((1,), (1,)), ((), ())), preferred_element_type=F32)


def _dot_tn(a, b):
    return lax.dot_general(a, b, (((0,), (0,)), ((), ())), preferred_element_type=F32)


def _inproj_kernel(x_ref, g_ref, wm_ref, wl_ref, wg_ref, qkvo_ref, lat_ref, gate_ref):
    x = x_ref[...]
    ms = jnp.mean(x * x, axis=-1, keepdims=True)
    h = (x * lax.rsqrt(ms + RMS_EPS) * g_ref[...]).astype(BF16)
    main = _dot(h, wm_ref[...])
    kcol = lax.broadcasted_iota(jnp.int32, (1, main.shape[1]), 1) // (M_H * M_DK) == 1
    main = jnp.where(kcol, main * (M_DK ** -0.5), main)
    qkvo_ref[...] = main.astype(BF16)
    lat_ref[...] = _dot(h, wl_ref[...])
    gate_ref[...] = _dot(h, wg_ref[...])


def _in_proj(x2, g, w_main, w_lat, w_gate, tm):
    T = x2.shape[0]
    n_main, n_lat, n_gate = w_main.shape[1], w_lat.shape[1], w_gate.shape[1]
    full = lambda shape: pl.BlockSpec(shape, lambda i: (0, 0))
    return pl.pallas_call(
        _inproj_kernel,
        grid=(T // tm,),
        in_specs=[pl.BlockSpec((tm, D_MODEL), lambda i: (i, 0)), full((1, D_MODEL)),
                  full(w_main.shape), full(w_lat.shape), full(w_gate.shape)],
        out_specs=[pl.BlockSpec((tm, n_main), lambda i: (i, 0)),
                   pl.BlockSpec((tm, n_lat), lambda i: (i, 0)),
                   pl.BlockSpec((tm, n_gate), lambda i: (i, 0))],
        out_shape=[jax.ShapeDtypeStruct((T, n_main), BF16),
                   jax.ShapeDtypeStruct((T, n_lat), F32),
                   jax.ShapeDtypeStruct((T, n_gate), F32)],
        compiler_params=_cparams(("parallel",)),
        name="in_proj",
    )(x2, g, w_main, w_lat, w_gate)


def _log_sigmoid(x):
    return -(jnp.maximum(-x, 0.0) + jnp.log1p(jnp.exp(-jnp.abs(x))))


def _mlstm_kernel(qkvo_ref, gate_ref, bias_ref, ng_ref, y_ref, c_ref, m_ref, *, L, nbatch):
    c_idx = pl.program_id(1)

    @pl.when(c_idx == 0)
    def _():
        c_ref[...] = jnp.zeros_like(c_ref)
        m_ref[...] = jnp.zeros_like(m_ref)

    H, DK, DV = M_H, M_DK, M_DV
    row = lax.broadcasted_iota(jnp.int32, (L, L), 0)
    col = lax.broadcasted_iota(jnp.int32, (L, L), 1)
    causal = col <= row
    tri = jnp.where(causal, 1.0, 0.0)
    lane = lax.broadcasted_iota(jnp.int32, (L, 128), 1)
    ones_col = jnp.where(lane == 0, 1.0, 0.0).astype(BF16)

    for bb in range(nbatch):
        G = gate_ref[bb] + bias_ref[...]
        LF = _log_sigmoid(G)
        Bc = jnp.dot(tri, LF, preferred_element_type=F32, precision=lax.Precision.HIGHEST)
        ZT = jnp.where(lane < H, G, Bc).T

        for h in range(H):
            q = qkvo_ref[bb, :, h * DK:(h + 1) * DK]
            k = qkvo_ref[bb, :, H * DK + h * DK: H * DK + (h + 1) * DK]
            v = qkvo_ref[bb, :, 2 * H * DK + h * DV: 2 * H * DK + (h + 1) * DV]
            o = qkvo_ref[bb, :, 3 * H * DK + h * DV: 3 * H * DK + (h + 1) * DV]
            i_col = G[:, h:h + 1]
            b_col = Bc[:, H + h:H + h + 1]
            i_row = ZT[h:h + 1, :]
            b_row = ZT[H + h:H + h + 1, :]
            m_prev = m_ref[bb * H + h]
            c_prev = c_ref[bb * H + h]

            a_col = b_col + m_prev
            D = jnp.where(causal, b_col - b_row + i_row, NEG_INF)
            m_t = jnp.maximum(a_col, jnp.max(D, axis=-1, keepdims=True))
            Dw = jnp.exp(D - m_t)
            aw = jnp.exp(a_col - m_t)
            s = (_dot_nt(q, k) * Dw).astype(BF16)
            v_aug = jnp.concatenate([v, ones_col], axis=-1)
            num = aw * _dot(q, c_prev.astype(BF16)) + _dot(s, v_aug)
            den = num[:, DV:DV + 1]
            hval = num[:, :DV] / jnp.maximum(jnp.abs(den), jnp.exp(-m_t))

            bL = b_col[L - 1:L, :]
            g_col = bL - b_col + i_col
            m_new = jnp.maximum(bL + m_prev, jnp.max(g_col, axis=0, keepdims=True))
            decay = jnp.exp(bL + m_prev - m_new)
            w_col = jnp.exp(g_col - m_new)
            wv = (w_col * v_aug.astype(F32)).astype(BF16)
            c_ref[bb * H + h] = decay * c_prev + _dot_tn(k, wv)
            m_ref[bb * H + h] = m_new

            ms = jnp.mean(hval * hval, axis=-1, keepdims=True)
            hn = hval * lax.rsqrt(ms + RMS_EPS) * ng_ref[:, h * DV:(h + 1) * DV]
            y_ref[bb, :, h * DV:(h + 1) * DV] = (jax.nn.sigmoid(o.astype(F32)) * hn).astype(BF16)


def _mlstm(qkvo, gates, bias_row, norm_g, B, S, L, nbatch):
    nc = S // L
    n_in = qkvo.shape[1]
    y = pl.pallas_call(
        functools.partial(_mlstm_kernel, L=L, nbatch=nbatch),
        grid=(B // nbatch, nc),
        in_specs=[pl.BlockSpec((nbatch, L, n_in), lambda b, c: (b, c, 0)),
                  pl.BlockSpec((nbatch, L, 128), lambda b, c: (b, c, 0)),
                  pl.BlockSpec((1, 128), lambda b, c: (0, 0)),
                  pl.BlockSpec((1, M_H * M_DV), lambda b, c: (0, 0))],
        out_specs=pl.BlockSpec((nbatch, L, M_H * M_DV), lambda b, c: (b, c, 0)),
        out_shape=jax.ShapeDtypeStruct((B, S, M_H * M_DV), BF16),
        scratch_shapes=[pltpu.VMEM((nbatch * M_H, M_DK, 2 * M_DV), F32),
                        pltpu.VMEM((nbatch * M_H, 1, 1), F32)],
        compiler_params=_cparams(("parallel", "arbitrary")),
        name="mlstm",
    )(qkvo.reshape(B, S, n_in), gates.reshape(B, S, 128), bias_row, norm_g)
    return y.reshape(B * S, M_H * M_DV)


def _mla_prep_kernel(lat_ref, pos_ref, cqg_ref, ckvg_ref, wq_ref, wqs_ref, wkv_ref,
                     qg_ref, qgs_ref, kg_ref, kgs_ref, freq_ref, sign_ref,
                     q_ref, k_ref, v_ref):
    H = A_H
    cq = lat_ref[:, :A_QR]
    ckv = lat_ref[:, A_QR:A_QR + A_KVR]
    kr = lat_ref[:, A_QR + A_KVR:A_QR + A_KVR + 128]
    krs = lat_ref[:, A_QR + A_KVR + 128:A_QR + A_KVR + 256]

    def rms(t, g):
        return t * lax.rsqrt(jnp.mean(t * t, axis=-1, keepdims=True) + RMS_EPS) * g

    cqn = rms(cq, cqg_ref[...]).astype(BF16)
    ckvn = rms(ckv, ckvg_ref[...]).astype(BF16)
    qf = _dot(cqn, wq_ref[...])
    qs = _dot(cqn, wqs_ref[...])
    kv = _dot(ckvn, wkv_ref[...])

    ang = pos_ref[...] * freq_ref[...]
    cos = jnp.cos(ang)
    sin = jnp.sin(ang) * sign_ref[...]
    scale = A_DQK ** -0.5
    kr_ss = jnp.sum(kr * kr, axis=-1, keepdims=True)

    for h in range(H):
        qh = qf[:, h * A_DPAD:(h + 1) * A_DPAD]
        rstd = lax.rsqrt(jnp.sum(qh * qh, axis=-1, keepdims=True) / A_DQK + RMS_EPS) * scale
        q_nope = qh[:, :128] * rstd * qg_ref[:, :128]
        q_rope = qh[:, 128:] * rstd * qg_ref[:, 128:]
        q_rope_s = qs[:, h * 128:(h + 1) * 128] * rstd * qgs_ref[...]
        q_ref[:, h * A_DPAD:h * A_DPAD + 128] = q_nope.astype(BF16)
        q_ref[:, h * A_DPAD + 128:(h + 1) * A_DPAD] = (q_rope * cos + q_rope_s * sin).astype(BF16)

        kn = kv[:, h * 256:h * 256 + 128]
        rstd_k = lax.rsqrt((jnp.sum(kn * kn, axis=-1, keepdims=True) + kr_ss) / A_DQK + RMS_EPS)
        k_nope = kn * rstd_k * kg_ref[:, :128]
        k_rope = kr * rstd_k * kg_ref[:, 128:]
        k_rope_s = krs * rstd_k * kgs_ref[...]
        k_ref[:, h * A_DPAD:h * A_DPAD + 128] = k_nope.astype(BF16)
        k_ref[:, h * A_DPAD + 128:(h + 1) * A_DPAD] = (k_rope * cos + k_rope_s * sin).astype(BF16)
        v_ref[:, h * A_DV:(h + 1) * A_DV] = kv[:, h * 256 + 128:(h + 1) * 256].astype(BF16)


def _mla_prep(lat, pos_col, consts, tm):
    T = lat.shape[0]
    full = lambda a: pl.BlockSpec(a.shape, lambda i: (0, 0))
    return pl.pallas_call(
        _mla_prep_kernel,
        grid=(T // tm,),
        in_specs=[pl.BlockSpec((tm, lat.shape[1]), lambda i: (i, 0)),
                  pl.BlockSpec((tm, 1), lambda i: (i, 0))] + [full(c) for c in consts],
        out_specs=[pl.BlockSpec((tm, A_H * A_DPAD), lambda i: (i, 0)),
                   pl.BlockSpec((tm, A_H * A_DPAD), lambda i: (i, 0)),
                   pl.BlockSpec((tm, A_H * A_DV), lambda i: (i, 0))],
        out_shape=[jax.ShapeDtypeStruct((T, A_H * A_DPAD), BF16),
                   jax.ShapeDtypeStruct((T, A_H * A_DPAD), BF16),
                   jax.ShapeDtypeStruct((T, A_H * A_DV), BF16)],
        compiler_params=_cparams(("parallel",)),
        name="mla_prep",
    )(lat, pos_col, *consts)


def _attn_kernel(q_ref, k_ref, v_ref, o_ref, *, tq, hp):
    qi = pl.program_id(2)

    def chunk(j, carry, masked):
        rows = pl.ds(pl.multiple_of(j * tq, tq), tq)
        out = []
        for h in range(hp):
            m, l, acc = carry[h]
            q = q_ref[:, h * A_DPAD:(h + 1) * A_DPAD]
            k = k_ref[rows, h * A_DPAD:(h + 1) * A_DPAD]
            v = v_ref[rows, h * A_DV:(h + 1) * A_DV]
            s = _dot_nt(q, k)
            if masked:
                row = lax.broadcasted_iota(jnp.int32, (tq, tq), 0)
                col = lax.broadcasted_iota(jnp.int32, (tq, tq), 1)
                s = jnp.where(col <= row, s, NEG_INF)
            m_new = jnp.maximum(m, jnp.max(s, axis=-1, keepdims=True))
            alpha = jnp.exp(m - m_new)
            p = jnp.exp(s - m_new)
            l = alpha * l + jnp.sum(p, axis=-1, keepdims=True)
            acc = alpha * acc + _dot(p.astype(BF16), v)
            out.append((m_new, l, acc))
        return tuple(out)

    init = tuple((jnp.full((tq, 1), NEG_INF, F32), jnp.zeros((tq, 1), F32), jnp.zeros((tq, A_DV), F32))
                 for _ in range(hp))
    carry = lax.fori_loop(0, qi, lambda j, c: chunk(j, c, False), init)
    carry = chunk(qi, carry, True)
    for h in range(hp):
        m, l, acc = carry[h]
        o_ref[:, h * A_DV:(h + 1) * A_DV] = (acc / l).astype(BF16)


def _mla_attn(q, k, v, B, S, tq, hp):
    T = B * S
    nq = S // tq
    return pl.pallas_call(
        functools.partial(_attn_kernel, tq=tq, hp=hp),
        grid=(B, A_H // hp, nq),
        in_specs=[pl.BlockSpec((tq, hp * A_DPAD), lambda b, h, i: (b * nq + i, h)),
                  pl.BlockSpec((S, hp * A_DPAD), lambda b, h, i: (b, h)),
                  pl.BlockSpec((S, hp * A_DV), lambda b, h, i: (b, h))],
        out_specs=pl.BlockSpec((tq, hp * A_DV), lambda b, h, i: (b * nq + i, h)),
        out_shape=jax.ShapeDtypeStruct((T, A_H * A_DV), BF16),
        compiler_params=_cparams(("parallel", "parallel", "arbitrary")),
        name="mla_attn",
    )(q, k, v)


def _outproj_kernel(x_ref, ym_ref, ya_ref, wom_ref, woa_ref, g_ref, wq_ref, k1_ref, k2_ref,
                    x1_ref, h2_ref, s1_ref, s2_ref):
    x1 = x_ref[...] + _dot(ym_ref[...], wom_ref[...]) + _dot(ya_ref[...], woa_ref[...])
    x1_ref[...] = x1
    ms = jnp.mean(x1 * x1, axis=-1, keepdims=True)
    h2f = x1 * lax.rsqrt(ms + RMS_EPS) * g_ref[...]
    h2 = h2f.astype(BF16)
    h2_ref[...] = h2
    qry = _dot(h2, wq_ref[...]).astype(BF16)
    half = P_DK // 2
    for h in range(P_H):
        s1_ref[h] = _dot_nt(k1_ref[...], qry[:, h * P_DK:h * P_DK + half])
        s2_ref[h] = _dot_nt(k2_ref[...], qry[:, h * P_DK + half:(h + 1) * P_DK])


def _out_proj(x2, ym, ya, wo_m, wo_a, g, wq, k1, k2, tm):
    T = x2.shape[0]
    full = lambda a: pl.BlockSpec(a.shape, lambda i: (0, 0))
    return pl.pallas_call(
        _outproj_kernel,
        grid=(T // tm,),
        in_specs=[pl.BlockSpec((tm, D_MODEL), lambda i: (i, 0)),
                  pl.BlockSpec((tm, ym.shape[1]), lambda i: (i, 0)),
                  pl.BlockSpec((tm, ya.shape[1]), lambda i: (i, 0)),
                  full(wo_m), full(wo_a), full(g), full(wq), full(k1), full(k2)],
        out_specs=[pl.BlockSpec((tm, D_MODEL), lambda i: (i, 0)),
                   pl.BlockSpec((tm, D_MODEL), lambda i: (i, 0)),
                   pl.BlockSpec((P_H, P_NK, tm), lambda i: (0, 0, i)),
                   pl.BlockSpec((P_H, P_NK, tm), lambda i: (0, 0, i))],
        out_shape=[jax.ShapeDtypeStruct((T, D_MODEL), F32),
                   jax.ShapeDtypeStruct((T, D_MODEL), BF16),
                   jax.ShapeDtypeStruct((P_H, P_NK, T), F32),
                   jax.ShapeDtypeStruct((P_H, P_NK, T), F32)],
        compiler_params=_cparams(("parallel",)),
        name="out_peerq",
    )(x2, ym, ya, wo_m, wo_a, g, wq, k1, k2)


def _top16(x):
    vals = []
    for _ in range(P_TOPK):
        m = jnp.max(x, axis=0, keepdims=True)
        vals.append(m)
        x = jnp.where(x == m, NEG_INF, x)
    return vals


def _oddeven_merge(lo, hi, r):
    step = r * 2
    if step < hi - lo:
        yield from _oddeven_merge(lo, hi, step)
        yield from _oddeven_merge(lo + r, hi, step)
        yield from [(i, i + r) for i in range(lo + r, hi - r, step)]
    else:
        yield (lo, lo + r)


def _oddeven_merge_sort(lo, hi):
    if hi - lo >= 1:
        mid = lo + (hi - lo) // 2
        yield from _oddeven_merge_sort(lo, mid)
        yield from _oddeven_merge_sort(mid + 1, hi)
        yield from _oddeven_merge(lo, hi, 1)


_SORT16 = tuple(_oddeven_merge_sort(0, P_TOPK - 1))
_SUBLANES = 8


def _cmpx(v, i, j):
    a, b = v[i], v[j]
    v[i] = jnp.maximum(a, b)
    v[j] = jnp.minimum(a, b)


def _top16_sorted(x):
    v = [x[i * _SUBLANES:(i + 1) * _SUBLANES, :] for i in range(P_TOPK)]
    for i, j in _SORT16:
        _cmpx(v, i, j)
    for shift in (4, 2, 1):
        o = [pltpu.roll(t, shift, axis=0) for t in v]
        v = [jnp.maximum(v[i], o[P_TOPK - 1 - i]) for i in range(P_TOPK)]
        for d in (8, 4, 2, 1):
            for i in range(P_TOPK):
                if not i & d:
                    _cmpx(v, i, i + d)
    return v


_CAND = [(a, b) for a in range(P_TOPK) for b in range(P_TOPK) if (a + 1) * (b + 1) <= P_TOPK]


def _route_kernel(s1_ref, s2_ref, r2_ref, e2_ref, cnt_ref, f_ref):
    s1 = s1_ref[0]
    s2 = s2_ref[0]
    ng = P_NK // _SUBLANES
    v1s = _top16_sorted(s1)
    v2s = _top16_sorted(s2)
    v1 = [t[0:1, :] for t in v1s]
    v2 = [t[0:1, :] for t in v2s]
    cand = jnp.concatenate([v1[a] + v2[b] for a, b in _CAND], axis=0)
    tau = _top16(cand)[P_TOPK - 1]
    top = v1[0] + v2[0]
    z = jnp.zeros_like(tau)
    cnt_a = [jnp.zeros_like(tau) for _ in range(P_TOPK)]
    for a, b in _CAND:
        sm = v1[a] + v2[b]
        sel = sm >= tau
        z = z + jnp.where(sel, jnp.exp(sm - top), 0.0)
        cnt_a[a] = cnt_a[a] + jnp.where(sel, 1.0, 0.0)
    cnt_b = [jnp.broadcast_to(c, (_SUBLANES, c.shape[1])) for c in cnt_a]
    for i in range(ng):
        rows = slice(i * _SUBLANES, (i + 1) * _SUBLANES)
        x1, x2 = s1[rows, :], s2[rows, :]
        r2 = jnp.full_like(x2, float(P_TOPK))
        cnt = jnp.zeros_like(x1)
        for a in reversed(range(P_TOPK)):
            r2 = jnp.where(x2 >= v2s[a], float(a), r2)
            cnt = jnp.where(x1 == v1s[a], cnt_b[a], cnt)
        r2_ref[0, rows, :] = r2
        cnt_ref[0, rows, :] = cnt
    e2_ref[0] = jnp.exp(s2 - v2[0])
    f_ref[0] = jnp.exp(s1 - v1[0]) / z


def _peer_route(s1t, s2t, tl):
    T = s1t.shape[2]
    spec = pl.BlockSpec((1, P_NK, tl), lambda h, i: (h, 0, i))
    shp32 = jax.ShapeDtypeStruct((P_H, P_NK, T), F32)
    return pl.pallas_call(
        _route_kernel,
        grid=(P_H, T // tl),
        in_specs=[spec, spec],
        out_specs=[spec, spec, spec, spec],
        out_shape=[shp32, shp32, shp32, shp32],
        compiler_params=_cparams(("parallel", "parallel")),
        name="peer_route",
    )(s1t, s2t)


def _gelu2(x):
    return x * (1.0 + lax.erf(x * (2.0 ** -0.5)))


_LANES = 128
_PACK = 16
_N1_GROUP = 4


def _peer_kernel(h2_ref, x1_ref, u_ref, v_ref, r2f_ref, e2f_ref, cnt_ref, f_ref, o_ref,
                 acc_ref, a_ref, p_ref, r2_ref, e2_ref, *, nb, tm):
    e = pl.program_id(1)

    @pl.when(e == 0)
    def _():
        acc_ref[...] = jnp.zeros_like(acc_ref)
        for h in range(P_H):
            r2_ref[h] = r2f_ref[h].astype(BF16)
            e2_ref[h] = e2f_ref[h].astype(BF16)

    a_ref[...] = _dot_nt(u_ref[...], h2_ref[...])
    n1_0 = pl.multiple_of(e * nb, nb)
    zero = jnp.zeros((_PACK, _LANES), BF16)
    for g in range(tm // _LANES):
        lanes = slice(g * _LANES, (g + 1) * _LANES)
        for jc in range(nb // _N1_GROUP):
            w = [[None] * (P_NK // _PACK) for _ in range(_N1_GROUP)]
            for h in range(P_H):
                cnt8 = cnt_ref[h, pl.ds(n1_0, nb), lanes]
                f8 = 0.5 * f_ref[h, pl.ds(n1_0, nb), lanes]
                r2t = [r2_ref[h, r * _PACK:(r + 1) * _PACK, lanes] for r in range(P_NK // _PACK)]
                e2t = [e2_ref[h, r * _PACK:(r + 1) * _PACK, lanes] for r in range(P_NK // _PACK)]
                for jj in range(_N1_GROUP):
                    j = jc * _N1_GROUP + jj
                    cb = jnp.broadcast_to(cnt8[j:j + 1, :], (_PACK, _LANES)).astype(BF16)
                    fb = jnp.broadcast_to(f8[j:j + 1, :], (_PACK, _LANES)).astype(BF16)
                    for r in range(P_NK // _PACK):
                        t = jnp.where(r2t[r] < cb, e2t[r], zero) * fb
                        w[jj][r] = t if w[jj][r] is None else w[jj][r] + t
            for jj in range(_N1_GROUP):
                j = jc * _N1_GROUP + jj
                for r in range(P_NK // _PACK):
                    rows = slice(j * P_NK + r * _PACK, j * P_NK + (r + 1) * _PACK)
                    p_ref[rows, lanes] = w[jj][r] * _gelu2(a_ref[rows, lanes]).astype(BF16)
    acc_ref[...] += _dot_tn(p_ref[...], v_ref[...])

    @pl.when(e == pl.num_programs(1) - 1)
    def _():
        o_ref[...] = x1_ref[...] + acc_ref[...]


def _peer_mlp(h2, x1, u, v, r2t, e2t, cntt, ft, tm, nb):
    T = x1.shape[0]
    ne = P_NK // nb
    te = nb * P_NK
    rspec = pl.BlockSpec((P_H, P_NK, tm), lambda i, e: (0, 0, i))
    return pl.pallas_call(
        functools.partial(_peer_kernel, nb=nb, tm=tm),
        grid=(T // tm, ne),
        in_specs=[pl.BlockSpec((tm, D_MODEL), lambda i, e: (i, 0)),
                  pl.BlockSpec((tm, D_MODEL), lambda i, e: (i, 0)),
                  pl.BlockSpec((te, D_MODEL), lambda i, e: (e, 0)),
                  pl.BlockSpec((te, D_MODEL), lambda i, e: (e, 0)),
                  rspec, rspec, rspec, rspec],
        out_specs=pl.BlockSpec((tm, D_MODEL), lambda i, e: (i, 0)),
        out_shape=jax.ShapeDtypeStruct((T, D_MODEL), F32),
        scratch_shapes=[pltpu.VMEM((tm, D_MODEL), F32),
                        pltpu.VMEM((te, tm), F32),
                        pltpu.VMEM((te, tm), BF16),
                        pltpu.VMEM((P_H, P_NK, tm), BF16),
                        pltpu.VMEM((P_H, P_NK, tm), BF16)],
        compiler_params=_cparams(("parallel", "arbitrary")),
        name="peer_mlp",
    )(h2, x1, u, v, r2t, e2t, cntt, ft)


def _swap_halves(t, axis=-1):
    a, b = jnp.split(t, 2, axis=axis)
    return jnp.concatenate([b, a], axis=axis)


def _pad_cols(t, n):
    return jnp.pad(t, ((0, 0), (0, n - t.shape[1])))


def _layer(x2, pos_col, B, S, attn_norm_g, w_in, b_igate, b_fgate, mlstm_norm_g, cq_norm_g, w_uq,
           ckv_norm_g, w_ukv, q_norm_g, k_norm_g, w_out, ffn_norm_g, peer_w_q, peer_keys_1,
           peer_keys_2, peer_u, peer_v):
    T = B * S
    n_qkvo = 4 * M_H * M_DK
    o_gate = n_qkvo
    o_cq = o_gate + 2 * M_H
    o_ckv = o_cq + A_QR
    o_kr = o_ckv + A_KVR

    w_main = w_in[:, :n_qkvo].astype(BF16)
    w_kr = w_in[:, o_kr:o_kr + A_ROPE]
    w_lat = jnp.concatenate([w_in[:, o_cq:o_kr], _pad_cols(w_kr, 128),
                             _pad_cols(_swap_halves(w_kr), 128)], axis=1).astype(BF16)
    w_gate = _pad_cols(w_in[:, o_gate:o_gate + 2 * M_H], 128).astype(BF16)
    bias_row = _pad_cols(jnp.concatenate([b_igate, b_fgate])[None, :], 128)

    wq_h = w_uq.reshape(A_QR, A_H, A_DQK)
    wq_pad = jnp.pad(wq_h, ((0, 0), (0, 0), (0, A_DPAD - A_DQK))).reshape(A_QR, A_H * A_DPAD).astype(BF16)
    wq_sw = jnp.pad(_swap_halves(wq_h[:, :, A_NOPE:]), ((0, 0), (0, 0), (0, 128 - A_ROPE)))
    wq_sw = wq_sw.reshape(A_QR, A_H * 128).astype(BF16)
    w_kv = w_ukv.astype(BF16)
    qg = _pad_cols(q_norm_g[None, :], A_DPAD)
    qgs = _pad_cols(_swap_halves(q_norm_g[A_NOPE:])[None, :], 128)
    kg = _pad_cols(k_norm_g[None, :], A_DPAD)
    kgs = _pad_cols(_swap_halves(k_norm_g[A_NOPE:])[None, :], 128)
    inv_freq = ROPE_THETA ** (-jnp.arange(0, A_ROPE, 2, dtype=F32) / A_ROPE)
    freq_row = _pad_cols(jnp.concatenate([inv_freq, inv_freq])[None, :], 128)
    half = A_ROPE // 2
    sign_row = _pad_cols(jnp.concatenate([-jnp.ones((half,), F32), jnp.ones((half,), F32)])[None, :], 128)

    qkvo, lat, gates = _in_proj(x2, attn_norm_g[None, :], w_main, w_lat, w_gate, tm=min(512, T))
    y_m = _mlstm(qkvo, gates, bias_row, mlstm_norm_g[None, :], B, S, L=min(256, S), nbatch=1)
    consts = [cq_norm_g[None, :], ckv_norm_g[None, :], wq_pad, wq_sw, w_kv, qg, qgs, kg, kgs,
              freq_row, sign_row]
    q, k, v = _mla_prep(lat, pos_col, consts, tm=min(512, T))
    y_a = _mla_attn(q, k, v, B, S, tq=min(512, S), hp=A_H)
    wo = w_out.astype(BF16)
    x1, h2, s1t, s2t = _out_proj(x2, y_m, y_a, wo[:M_H * M_DV], wo[M_H * M_DV:], ffn_norm_g[None, :],
                                  peer_w_q.astype(BF16), peer_keys_1.astype(BF16),
                                  peer_keys_2.astype(BF16), tm=min(512, T))
    r2t, e2t, cntt, ft = _peer_route(s1t, s2t, tl=min(256, T))
    return _peer_mlp(h2, x1, peer_u.astype(BF16), peer_v.astype(BF16), r2t, e2t, cntt, ft,
                     tm=min(512, T), nb=8)


def kernel(x, positions, attn_norm_g, w_in, b_igate, b_fgate, mlstm_norm_g, cq_norm_g, w_uq, ckv_norm_g,
           w_ukv, q_norm_g, k_norm_g, w_out, ffn_norm_g, peer_w_q, peer_keys_1, peer_keys_2, peer_u, peer_v):
    B, S, D = x.shape
    x2 = x.reshape(B * S, D)
    pos_col = positions.reshape(B * S, 1).astype(F32)
    for l in range(attn_norm_g.shape[0]):
        x2 = _layer(x2, pos_col, B, S, attn_norm_g[l], w_in[l], b_igate[l], b_fgate[l], mlstm_norm_g[l],
                    cq_norm_g[l], w_uq[l], ckv_norm_g[l], w_ukv[l], q_norm_g[l], k_norm_g[l], w_out[l],
                    ffn_norm_g[l], peer_w_q[l], peer_keys_1[l], peer_keys_2[l], peer_u[l], peer_v[l])
    return x2.reshape(B, S, D)
```

```python
import functools

import jax
import jax.numpy as jnp
from jax import lax
from jax.experimental import pallas as pl
from jax.experimental.pallas import tpu as pltpu

F32 = jnp.float32
BF16 = jnp.bfloat16

D_MODEL = 1024
RMS_EPS = 1e-6
M_H, M_DK, M_DV = 4, 128, 128
A_H, A_QR, A_KVR, A_NOPE, A_ROPE, A_DV = 4, 256, 128, 128, 64, 128
A_DQK = A_NOPE + A_ROPE
A_DPAD = 256
ROPE_THETA = 10000.0
P_H, P_NK, P_DK, P_TOPK = 8, 128, 256, 16
NEG_INF = float("-inf")

VMEM_LIMIT = 56 * 1024 * 1024


def _cparams(sem):
    return pltpu.CompilerParams(dimension_semantics=sem, vmem_limit_bytes=VMEM_LIMIT)


def _dot(a, b):
    return jnp.dot(a, b, preferred_element_type=F32)


def _dot_nt(a, b):
    return lax.dot_general(a, b, (((1,), (1,)), ((), ())), preferred_element_type=F32)


def _dot_tn(a, b):
    return lax.dot_general(a, b, (((0,), (0,)), ((), ())), preferred_element_type=F32)


def _inproj_kernel(x_ref, g_ref, wm_ref, wl_ref, wg_ref, qkvo_ref, lat_ref, gate_ref):
    x = x_ref[...]
    ms = jnp.mean(x * x, axis=-1, keepdims=True)
    h = (x * lax.rsqrt(ms + RMS_EPS) * g_ref[...]).astype(BF16)
    main = _dot(h, wm_ref[...])
    kcol = lax.broadcasted_iota(jnp.int32, (1, main.shape[1]), 1) // (M_H * M_DK) == 1
    main = jnp.where(kcol, main * (M_DK ** -0.5), main)
    qkvo_ref[...] = main.astype(BF16)
    lat_ref[...] = _dot(h, wl_ref[...])
    gate_ref[...] = _dot(h, wg_ref[...])


def _in_proj(x2, g, w_main, w_lat, w_gate, tm):
    T = x2.shape[0]
    n_main, n_lat, n_gate = w_main.shape[1], w_lat.shape[1], w_gate.shape[1]
    full = lambda shape: pl.BlockSpec(shape, lambda i: (0, 0))
    return pl.pallas_call(
        _inproj_kernel,
        grid=(T // tm,),
        in_specs=[pl.BlockSpec((tm, D_MODEL), lambda i: (i, 0)), full((1, D_MODEL)),
                  full(w_main.shape), full(w_lat.shape), full(w_gate.shape)],
        out_specs=[pl.BlockSpec((tm, n_main), lambda i: (i, 0)),
                   pl.BlockSpec((tm, n_lat), lambda i: (i, 0)),
                   pl.BlockSpec((tm, n_gate), lambda i: (i, 0))],
        out_shape=[jax.ShapeDtypeStruct((T, n_main), BF16),
                   jax.ShapeDtypeStruct((T, n_lat), F32),
                   jax.ShapeDtypeStruct((T, n_gate), F32)],
        compiler_params=_cparams(("parallel",)),
        name="in_proj",
    )(x2, g, w_main, w_lat, w_gate)


def _log_sigmoid(x):
    return -(jnp.maximum(-x, 0.0) + jnp.log1p(jnp.exp(-jnp.abs(x))))


def _mlstm_kernel(qkvo_ref, gate_ref, bias_ref, ng_ref, y_ref, c_ref, m_ref, *, L, nbatch):
    c_idx = pl.program_id(1)

    @pl.when(c_idx == 0)
    def _():
        c_ref[...] = jnp.zeros_like(c_ref)
        m_ref[...] = jnp.zeros_like(m_ref)

    H, DK, DV = M_H, M_DK, M_DV
    row = lax.broadcasted_iota(jnp.int32, (L, L), 0)
    col = lax.broadcasted_iota(jnp.int32, (L, L), 1)
    causal = col <= row
    tri = jnp.where(causal, 1.0, 0.0)
    lane = lax.broadcasted_iota(jnp.int32, (L, 128), 1)
    ones_col = jnp.where(lane == 0, 1.0, 0.0).astype(BF16)

    for bb in range(nbatch):
        G = gate_ref[bb] + bias_ref[...]
        LF = _log_sigmoid(G)
        Bc = jnp.dot(tri, LF, preferred_element_type=F32, precision=lax.Precision.HIGHEST)
        ZT = jnp.where(lane < H, G, Bc).T

        for h in range(H):
            q = qkvo_ref[bb, :, h * DK:(h + 1) * DK]
            k = qkvo_ref[bb, :, H * DK + h * DK: H * DK + (h + 1) * DK]
            v = qkvo_ref[bb, :, 2 * H * DK + h * DV: 2 * H * DK + (h + 1) * DV]
            o = qkvo_ref[bb, :, 3 * H * DK + h * DV: 3 * H * DK + (h + 1) * DV]
            i_col = G[:, h:h + 1]
            b_col = Bc[:, H + h:H + h + 1]
            i_row = ZT[h:h + 1, :]
            b_row = ZT[H + h:H + h + 1, :]
            m_prev = m_ref[bb * H + h]
            c_prev = c_ref[bb * H + h]

            a_col = b_col + m_prev
            D = jnp.where(causal, b_col - b_row + i_row, NEG_INF)
            m_t = jnp.maximum(a_col, jnp.max(D, axis=-1, keepdims=True))
            Dw = jnp.exp(D - m_t)
            aw = jnp.exp(a_col - m_t)
            s = (_dot_nt(q, k) * Dw).astype(BF16)
            v_aug = jnp.concatenate([v, ones_col], axis=-1)
            num = aw * _dot(q, c_prev.astype(BF16)) + _dot(s, v_aug)
            den = num[:, DV:DV + 1]
            hval = num[:, :DV] / jnp.maximum(jnp.abs(den), jnp.exp(-m_t))

            bL = b_col[L - 1:L, :]
            g_col = bL - b_col + i_col
            m_new = jnp.maximum(bL + m_prev, jnp.max(g_col, axis=0, keepdims=True))
            decay = jnp.exp(bL + m_prev - m_new)
            w_col = jnp.exp(g_col - m_new)
            wv = (w_col * v_aug.astype(F32)).astype(BF16)
            c_ref[bb * H + h] = decay * c_prev + _dot_tn(k, wv)
            m_ref[bb * H + h] = m_new

            ms = jnp.mean(hval * hval, axis=-1, keepdims=True)
            hn = hval * lax.rsqrt(ms + RMS_EPS) * ng_ref[:, h * DV:(h + 1) * DV]
            y_ref[bb, :, h * DV:(h + 1) * DV] = (jax.nn.sigmoid(o.astype(F32)) * hn).astype(BF16)


def _mlstm(qkvo, gates, bias_row, norm_g, B, S, L, nbatch):
    nc = S // L
    n_in = qkvo.shape[1]
    y = pl.pallas_call(
        functools.partial(_mlstm_kernel, L=L, nbatch=nbatch),
        grid=(B // nbatch, nc),
        in_specs=[pl.BlockSpec((nbatch, L, n_in), lambda b, c: (b, c, 0)),
                  pl.BlockSpec((nbatch, L, 128), lambda b, c: (b, c, 0)),
                  pl.BlockSpec((1, 128), lambda b, c: (0, 0)),
                  pl.BlockSpec((1, M_H * M_DV), lambda b, c: (0, 0))],
        out_specs=pl.BlockSpec((nbatch, L, M_H * M_DV), lambda b, c: (b, c, 0)),
        out_shape=jax.ShapeDtypeStruct((B, S, M_H * M_DV), BF16),
        scratch_shapes=[pltpu.VMEM((nbatch * M_H, M_DK, 2 * M_DV), F32),
                        pltpu.VMEM((nbatch * M_H, 1, 1), F32)],
        compiler_params=_cparams(("parallel", "arbitrary")),
        name="mlstm",
    )(qkvo.reshape(B, S, n_in), gates.reshape(B, S, 128), bias_row, norm_g)
    return y.reshape(B * S, M_H * M_DV)


def _mla_prep_kernel(lat_ref, pos_ref, cqg_ref, ckvg_ref, wq_ref, wqs_ref, wkv_ref,
                     qg_ref, qgs_ref, kg_ref, kgs_ref, freq_ref, sign_ref,
                     q_ref, k_ref, v_ref):
    H = A_H
    cq = lat_ref[:, :A_QR]
    ckv = lat_ref[:, A_QR:A_QR + A_KVR]
    kr = lat_ref[:, A_QR + A_KVR:A_QR + A_KVR + 128]
    krs = lat_ref[:, A_QR + A_KVR + 128:A_QR + A_KVR + 256]

    def rms(t, g):
        return t * lax.rsqrt(jnp.mean(t * t, axis=-1, keepdims=True) + RMS_EPS) * g

    cqn = rms(cq, cqg_ref[...]).astype(BF16)
    ckvn = rms(ckv, ckvg_ref[...]).astype(BF16)
    qf = _dot(cqn, wq_ref[...])
    qs = _dot(cqn, wqs_ref[...])
    kv = _dot(ckvn, wkv_ref[...])

    ang = pos_ref[...] * freq_ref[...]
    cos = jnp.cos(ang)
    sin = jnp.sin(ang) * sign_ref[...]
    scale = A_DQK ** -0.5
    kr_ss = jnp.sum(kr * kr, axis=-1, keepdims=True)

    for h in range(H):
        qh = qf[:, h * A_DPAD:(h + 1) * A_DPAD]
        rstd = lax.rsqrt(jnp.sum(qh * qh, axis=-1, keepdims=True) / A_DQK + RMS_EPS) * scale
        q_nope = qh[:, :128] * rstd * qg_ref[:, :128]
        q_rope = qh[:, 128:] * rstd * qg_ref[:, 128:]
        q_rope_s = qs[:, h * 128:(h + 1) * 128] * rstd * qgs_ref[...]
        q_ref[:, h * A_DPAD:h * A_DPAD + 128] = q_nope.astype(BF16)
        q_ref[:, h * A_DPAD + 128:(h + 1) * A_DPAD] = (q_rope * cos + q_rope_s * sin).astype(BF16)

        kn = kv[:, h * 256:h * 256 + 128]
        rstd_k = lax.rsqrt((jnp.sum(kn * kn, axis=-1, keepdims=True) + kr_ss) / A_DQK + RMS_EPS)
        k_nope = kn * rstd_k * kg_ref[:, :128]
        k_rope = kr * rstd_k * kg_ref[:, 128:]
        k_rope_s = krs * rstd_k * kgs_ref[...]
        k_ref[:, h * A_DPAD:h * A_DPAD + 128] = k_nope.astype(BF16)
        k_ref[:, h * A_DPAD + 128:(h + 1) * A_DPAD] = (k_rope * cos + k_rope_s * sin).astype(BF16)
        v_ref[:, h * A_DV:(h + 1) * A_DV] = kv[:, h * 256 + 128:(h + 1) * 256].astype(BF16)


def _mla_prep(lat, pos_col, consts, tm):
    T = lat.shape[0]
    full = lambda a: pl.BlockSpec(a.shape, lambda i: (0, 0))
    return pl.pallas_call(
        _mla_prep_kernel,
        grid=(T // tm,),
        in_specs=[pl.BlockSpec((tm, lat.shape[1]), lambda i: (i, 0)),
                  pl.BlockSpec((tm, 1), lambda i: (i, 0))] + [full(c) for c in consts],
        out_specs=[pl.BlockSpec((tm, A_H * A_DPAD), lambda i: (i, 0)),
                   pl.BlockSpec((tm, A_H * A_DPAD), lambda i: (i, 0)),
                   pl.BlockSpec((tm, A_H * A_DV), lambda i: (i, 0))],
        out_shape=[jax.ShapeDtypeStruct((T, A_H * A_DPAD), BF16),
                   jax.ShapeDtypeStruct((T, A_H * A_DPAD), BF16),
                   jax.ShapeDtypeStruct((T, A_H * A_DV), BF16)],
        compiler_params=_cparams(("parallel",)),
        name="mla_prep",
    )(lat, pos_col, *consts)


def _attn_kernel(q_ref, k_ref, v_ref, o_ref, *, tq, hp):
    qi = pl.program_id(2)

    def chunk(j, carry, masked):
        rows = pl.ds(pl.multiple_of(j * tq, tq), tq)
        out = []
        for h in range(hp):
            m, l, acc = carry[h]
            q = q_ref[:, h * A_DPAD:(h + 1) * A_DPAD]
            k = k_ref[rows, h * A_DPAD:(h + 1) * A_DPAD]
            v = v_ref[rows, h * A_DV:(h + 1) * A_DV]
            s = _dot_nt(q, k)
            if masked:
                row = lax.broadcasted_iota(jnp.int32, (tq, tq), 0)
                col = lax.broadcasted_iota(jnp.int32, (tq, tq), 1)
                s = jnp.where(col <= row, s, NEG_INF)
            m_new = jnp.maximum(m, jnp.max(s, axis=-1, keepdims=True))
            alpha = jnp.exp(m - m_new)
            p = jnp.exp(s - m_new)
            l = alpha * l + jnp.sum(p, axis=-1, keepdims=True)
            acc = alpha * acc + _dot(p.astype(BF16), v)
            out.append((m_new, l, acc))
        return tuple(out)

    init = tuple((jnp.full((tq, 1), NEG_INF, F32), jnp.zeros((tq, 1), F32), jnp.zeros((tq, A_DV), F32))
                 for _ in range(hp))
    carry = lax.fori_loop(0, qi, lambda j, c: chunk(j, c, False), init)
    carry = chunk(qi, carry, True)
    for h in range(hp):
        m, l, acc = carry[h]
        o_ref[:, h * A_DV:(h + 1) * A_DV] = (acc / l).astype(BF16)


def _mla_attn(q, k, v, B, S, tq, hp):
    T = B * S
    nq = S // tq
    return pl.pallas_call(
        functools.partial(_attn_kernel, tq=tq, hp=hp),
        grid=(B, A_H // hp, nq),
        in_specs=[pl.BlockSpec((tq, hp * A_DPAD), lambda b, h, i: (b * nq + i, h)),
                  pl.BlockSpec((S, hp * A_DPAD), lambda b, h, i: (b, h)),
                  pl.BlockSpec((S, hp * A_DV), lambda b, h, i: (b, h))],
        out_specs=pl.BlockSpec((tq, hp * A_DV), lambda b, h, i: (b * nq + i, h)),
        out_shape=jax.ShapeDtypeStruct((T, A_H * A_DV), BF16),
        compiler_params=_cparams(("parallel", "parallel", "arbitrary")),
        name="mla_attn",
    )(q, k, v)


def _outproj_kernel(x_ref, ym_ref, ya_ref, wom_ref, woa_ref, g_ref, wq_ref, k1_ref, k2_ref,
                    x1_ref, h2_ref, s1_ref, s2_ref):
    x1 = x_ref[...] + _dot(ym_ref[...], wom_ref[...]) + _dot(ya_ref[...], woa_ref[...])
    x1_ref[...] = x1
    ms = jnp.mean(x1 * x1, axis=-1, keepdims=True)
    h2f = x1 * lax.rsqrt(ms + RMS_EPS) * g_ref[...]
    h2 = h2f.astype(BF16)
    h2_ref[...] = h2
    qry = _dot(h2, wq_ref[...]).astype(BF16)
    half = P_DK // 2
    for h in range(P_H):
        s1_ref[h] = _dot_nt(k1_ref[...], qry[:, h * P_DK:h * P_DK + half])
        s2_ref[h] = _dot_nt(k2_ref[...], qry[:, h * P_DK + half:(h + 1) * P_DK])


def _out_proj(x2, ym, ya, wo_m, wo_a, g, wq, k1, k2, tm):
    T = x2.shape[0]
    full = lambda a: pl.BlockSpec(a.shape, lambda i: (0, 0))
    return pl.pallas_call(
        _outproj_kernel,
        grid=(T // tm,),
        in_specs=[pl.BlockSpec((tm, D_MODEL), lambda i: (i, 0)),
                  pl.BlockSpec((tm, ym.shape[1]), lambda i: (i, 0)),
                  pl.BlockSpec((tm, ya.shape[1]), lambda i: (i, 0)),
                  full(wo_m), full(wo_a), full(g), full(wq), full(k1), full(k2)],
        out_specs=[pl.BlockSpec((tm, D_MODEL), lambda i: (i, 0)),
                   pl.BlockSpec((tm, D_MODEL), lambda i: (i, 0)),
                   pl.BlockSpec((P_H, P_NK, tm), lambda i: (0, 0, i)),
                   pl.BlockSpec((P_H, P_NK, tm), lambda i: (0, 0, i))],
        out_shape=[jax.ShapeDtypeStruct((T, D_MODEL), F32),
                   jax.ShapeDtypeStruct((T, D_MODEL), BF16),
                   jax.ShapeDtypeStruct((P_H, P_NK, T), F32),
                   jax.ShapeDtypeStruct((P_H, P_NK, T), F32)],
        compiler_params=_cparams(("parallel",)),
        name="out_peerq",
    )(x2, ym, ya, wo_m, wo_a, g, wq, k1, k2)


def _top16(x):
    vals = []
    for _ in range(P_TOPK):
        m = jnp.max(x, axis=0, keepdims=True)
        vals.append(m)
        x = jnp.where(x == m, NEG_INF, x)
    return vals


def _oddeven_merge(lo, hi, r):
    step = r * 2
    if step < hi - lo:
        yield from _oddeven_merge(lo, hi, step)
        yield from _oddeven_merge(lo + r, hi, step)
        yield from [(i, i + r) for i in range(lo + r, hi - r, step)]
    else:
        yield (lo, lo + r)


def _oddeven_merge_sort(lo, hi):
    if hi - lo >= 1:
        mid = lo + (hi - lo) // 2
        yield from _oddeven_merge_sort(lo, mid)
        yield from _oddeven_merge_sort(mid + 1, hi)
        yield from _oddeven_merge(lo, hi, 1)


_SORT16 = tuple(_oddeven_merge_sort(0, P_TOPK - 1))
_SUBLANES = 8


def _cmpx(v, i, j):
    a, b = v[i], v[j]
    v[i] = jnp.maximum(a, b)
    v[j] = jnp.minimum(a, b)


def _top16_sorted(x):
    v = [x[i * _SUBLANES:(i + 1) * _SUBLANES, :] for i in range(P_TOPK)]
    for i, j in _SORT16:
        _cmpx(v, i, j)
    for shift in (4, 2, 1):
        o = [pltpu.roll(t, shift, axis=0) for t in v]
        v = [jnp.maximum(v[i], o[P_TOPK - 1 - i]) for i in range(P_TOPK)]
        for d in (8, 4, 2, 1):
            for i in range(P_TOPK):
                if not i & d:
                    _cmpx(v, i, i + d)
    return v


_CAND = [(a, b) for a in range(P_TOPK) for b in range(P_TOPK) if (a + 1) * (b + 1) <= P_TOPK]


def _route_kernel(s1_ref, s2_ref, r2_ref, e2_ref, cnt_ref, f_ref):
    s1 = s1_ref[0]
    s2 = s2_ref[0]
    ng = P_NK // _SUBLANES
    v1s = _top16_sorted(s1)
    v2s = _top16_sorted(s2)
    v1 = [t[0:1, :] for t in v1s]
    v2 = [t[0:1, :] for t in v2s]
    cand = jnp.concatenate([v1[a] + v2[b] for a, b in _CAND], axis=0)
    tau = _top16(cand)[P_TOPK - 1]
    top = v1[0] + v2[0]
    z = jnp.zeros_like(tau)
    cnt_a = [jnp.zeros_like(tau) for _ in range(P_TOPK)]
    for a, b in _CAND:
        sm = v1[a] + v2[b]
        sel = sm >= tau
        z = z + jnp.where(sel, jnp.exp(sm - top), 0.0)
        cnt_a[a] = cnt_a[a] + jnp.where(sel, 1.0, 0.0)
    cnt_b = [jnp.broadcast_to(c, (_SUBLANES, c.shape[1])) for c in cnt_a]
    for i in range(ng):
        rows = slice(i * _SUBLANES, (i + 1) * _SUBLANES)
        x1, x2 = s1[rows, :], s2[rows, :]
        r2 = jnp.full_like(x2, float(P_TOPK))
        cnt = jnp.zeros_like(x1)
        for a in reversed(range(P_TOPK)):
            r2 = jnp.where(x2 >= v2s[a], float(a), r2)
            cnt = jnp.where(x1 == v1s[a], cnt_b[a], cnt)
        r2_ref[0, rows, :] = r2
        cnt_ref[0, rows, :] = cnt
    e2_ref[0] = jnp.exp(s2 - v2[0])
    f_ref[0] = jnp.exp(s1 - v1[0]) / z


def _peer_route(s1t, s2t, tl):
    T = s1t.shape[2]
    spec = pl.BlockSpec((1, P_NK, tl), lambda h, i: (h, 0, i))
    shp32 = jax.ShapeDtypeStruct((P_H, P_NK, T), F32)
    return pl.pallas_call(
        _route_kernel,
        grid=(P_H, T // tl),
        in_specs=[spec, spec],
        out_specs=[spec, spec, spec, spec],
        out_shape=[shp32, shp32, shp32, shp32],
        compiler_params=_cparams(("parallel", "parallel")),
        name="peer_route",
    )(s1t, s2t)


def _gelu2(x):
    return x * (1.0 + lax.erf(x * (2.0 ** -0.5)))


_LANES = 128
_PACK = 16
_N1_GROUP = 4


def _peer_kernel(h2_ref, x1_ref, u_ref, v_ref, r2f_ref, e2f_ref, cnt_ref, f_ref, o_ref,
                 acc_ref, a_ref, p_ref, r2_ref, e2_ref, *, nb, tm):
    e = pl.program_id(1)

    @pl.when(e == 0)
    def _():
        acc_ref[...] = jnp.zeros_like(acc_ref)
        for h in range(P_H):
            r2_ref[h] = r2f_ref[h].astype(BF16)
            e2_ref[h] = e2f_ref[h].astype(BF16)

    a_ref[...] = _dot_nt(u_ref[...], h2_ref[...])
    n1_0 = pl.multiple_of(e * nb, nb)
    zero = jnp.zeros((_PACK, _LANES), BF16)
    for g in range(tm // _LANES):
        lanes = slice(g * _LANES, (g + 1) * _LANES)
        for jc in range(nb // _N1_GROUP):
            w = [[None] * (P_NK // _PACK) for _ in range(_N1_GROUP)]
            for h in range(P_H):
                cnt8 = cnt_ref[h, pl.ds(n1_0, nb), lanes]
                f8 = 0.5 * f_ref[h, pl.ds(n1_0, nb), lanes]
                r2t = [r2_ref[h, r * _PACK:(r + 1) * _PACK, lanes] for r in range(P_NK // _PACK)]
                e2t = [e2_ref[h, r * _PACK:(r + 1) * _PACK, lanes] for r in range(P_NK // _PACK)]
                for jj in range(_N1_GROUP):
                    j = jc * _N1_GROUP + jj
                    cb = jnp.broadcast_to(cnt8[j:j + 1, :], (_PACK, _LANES)).astype(BF16)
                    fb = jnp.broadcast_to(f8[j:j + 1, :], (_PACK, _LANES)).astype(BF16)
                    for r in range(P_NK // _PACK):
                        t = jnp.where(r2t[r] < cb, e2t[r], zero) * fb
                        w[jj][r] = t if w[jj][r] is None else w[jj][r] + t
            for jj in range(_N1_GROUP):
                j = jc * _N1_GROUP + jj
                for r in range(P_NK // _PACK):
                    rows = slice(j * P_NK + r * _PACK, j * P_NK + (r + 1) * _PACK)
                    p_ref[rows, lanes] = w[jj][r] * _gelu2(a_ref[rows, lanes]).astype(BF16)
    acc_ref[...] += _dot_tn(p_ref[...], v_ref[...])

    @pl.when(e == pl.num_programs(1) - 1)
    def _():
        o_ref[...] = x1_ref[...] + acc_ref[...]


def _peer_mlp(h2, x1, u, v, r2t, e2t, cntt, ft, tm, nb):
    T = x1.shape[0]
    ne = P_NK // nb
    te = nb * P_NK
    rspec = pl.BlockSpec((P_H, P_NK, tm), lambda i, e: (0, 0, i))
    return pl.pallas_call(
        functools.partial(_peer_kernel, nb=nb, tm=tm),
        grid=(T // tm, ne),
        in_specs=[pl.BlockSpec((tm, D_MODEL), lambda i, e: (i, 0)),
                  pl.BlockSpec((tm, D_MODEL), lambda i, e: (i, 0)),
                  pl.BlockSpec((te, D_MODEL), lambda i, e: (e, 0)),
                  pl.BlockSpec((te, D_MODEL), lambda i, e: (e, 0)),
                  rspec, rspec, rspec, rspec],
        out_specs=pl.BlockSpec((tm, D_MODEL), lambda i, e: (i, 0)),
        out_shape=jax.ShapeDtypeStruct((T, D_MODEL), F32),
        scratch_shapes=[pltpu.VMEM((tm, D_MODEL), F32),
                        pltpu.VMEM((te, tm), F32),
                        pltpu.VMEM((te, tm), BF16),
                        pltpu.VMEM((P_H, P_NK, tm), BF16),
                        pltpu.VMEM((P_H, P_NK, tm), BF16)],
        compiler_params=_cparams(("parallel", "arbitrary")),
        name="peer_mlp",
    )(h2, x1, u, v, r2t, e2t, cntt, ft)


def _swap_halves(t, axis=-1):
    a, b = jnp.split(t, 2, axis=axis)
    return jnp.concatenate([b, a], axis=axis)


def _pad_cols(t, n):
    return jnp.pad(t, ((0, 0), (0, n - t.shape[1])))


def _layer(x2, pos_col, B, S, attn_norm_g, w_in, b_igate, b_fgate, mlstm_norm_g, cq_norm_g, w_uq,
           ckv_norm_g, w_ukv, q_norm_g, k_norm_g, w_out, ffn_norm_g, peer_w_q, peer_keys_1,
           peer_keys_2, peer_u, peer_v):
    T = B * S
    n_qkvo = 4 * M_H * M_DK
    o_gate = n_qkvo
    o_cq = o_gate + 2 * M_H
    o_ckv = o_cq + A_QR
    o_kr = o_ckv + A_KVR

    w_main = w_in[:, :n_qkvo].astype(BF16)
    w_kr = w_in[:, o_kr:o_kr + A_ROPE]
    w_lat = jnp.concatenate([w_in[:, o_cq:o_kr], _pad_cols(w_kr, 128),
                             _pad_cols(_swap_halves(w_kr), 128)], axis=1).astype(BF16)
    w_gate = _pad_cols(w_in[:, o_gate:o_gate + 2 * M_H], 128).astype(BF16)
    bias_row = _pad_cols(jnp.concatenate([b_igate, b_fgate])[None, :], 128)

    wq_h = w_uq.reshape(A_QR, A_H, A_DQK)
    wq_pad = jnp.pad(wq_h, ((0, 0), (0, 0), (0, A_DPAD - A_DQK))).reshape(A_QR, A_H * A_DPAD).astype(BF16)
    wq_sw = jnp.pad(_swap_halves(wq_h[:, :, A_NOPE:]), ((0, 0), (0, 0), (0, 128 - A_ROPE)))
    wq_sw = wq_sw.reshape(A_QR, A_H * 128).astype(BF16)
    w_kv = w_ukv.astype(BF16)
    qg = _pad_cols(q_norm_g[None, :], A_DPAD)
    qgs = _pad_cols(_swap_halves(q_norm_g[A_NOPE:])[None, :], 128)
    kg = _pad_cols(k_norm_g[None, :], A_DPAD)
    kgs = _pad_cols(_swap_halves(k_norm_g[A_NOPE:])[None, :], 128)
    inv_freq = ROPE_THETA ** (-jnp.arange(0, A_ROPE, 2, dtype=F32) / A_ROPE)
    freq_row = _pad_cols(jnp.concatenate([inv_freq, inv_freq])[None, :], 128)
    half = A_ROPE // 2
    sign_row = _pad_cols(jnp.concatenate([-jnp.ones((half,), F32), jnp.ones((half,), F32)])[None, :], 128)

    qkvo, lat, gates = _in_proj(x2, attn_norm_g[None, :], w_main, w_lat, w_gate, tm=min(512, T))
    y_m = _mlstm(qkvo, gates, bias_row, mlstm_norm_g[None, :], B, S, L=min(256, S), nbatch=1)
    consts = [cq_norm_g[None, :], ckv_norm_g[None, :], wq_pad, wq_sw, w_kv, qg, qgs, kg, kgs,
              freq_row, sign_row]
    q, k, v = _mla_prep(lat, pos_col, consts, tm=min(512, T))
    y_a = _mla_attn(q, k, v, B, S, tq=min(1024, S), hp=2)
    wo = w_out.astype(BF16)
    x1, h2, s1t, s2t = _out_proj(x2, y_m, y_a, wo[:M_H * M_DV], wo[M_H * M_DV:], ffn_norm_g[None, :],
                                  peer_w_q.astype(BF16), peer_keys_1.astype(BF16),
                                  peer_keys_2.astype(BF16), tm=min(512, T))
    r2t, e2t, cntt, ft = _peer_route(s1t, s2t, tl=min(256, T))
    return _peer_mlp(h2, x1, peer_u.astype(BF16), peer_v.astype(BF16), r2t, e2t, cntt, ft,
                     tm=min(512, T), nb=8)


def kernel(x, positions, attn_norm_g, w_in, b_igate, b_fgate, mlstm_norm_g, cq_norm_g, w_uq, ckv_norm_g,
           w_ukv, q_norm_g, k_norm_g, w_out, ffn_norm_g, peer_w_q, peer_keys_1, peer_keys_2, peer_u, peer_v):
    B, S, D = x.shape
    x2 = x.reshape(B * S, D)
    pos_col = positions.reshape(B * S, 1).astype(F32)
    for l in range(attn_norm_g.shape[0]):
        x2 = _layer(x2, pos_col, B, S, attn_norm_g[l], w_in[l], b_igate[l], b_fgate[l], mlstm_norm_g[l],
                    cq_norm_g[l], w_uq[l], ckv_norm_g[l], w_ukv[l], q_norm_g[l], k_norm_g[l], w_out[l],
                    ffn_norm_g[l], peer_w_q[l], peer_keys_1[l], peer_keys_2[l], peer_u[l], peer_v[l])
    return x2.reshape(B, S, D)
```

```python
import functools

import jax
import jax.numpy as jnp
from jax import lax
from jax.experimental import pallas as pl
from jax.experimental.pallas import tpu as pltpu

F32 = jnp.float32
BF16 = jnp.bfloat16

D_MODEL = 1024
RMS_EPS = 1e-6
M_H, M_DK, M_DV = 4, 128, 128
A_H, A_QR, A_KVR, A_NOPE, A_ROPE, A_DV = 4, 256, 128, 128, 64, 128
A_DQK = A_NOPE + A_ROPE
A_DPAD = 256
ROPE_THETA = 10000.0
P_H, P_NK, P_DK, P_TOPK = 8, 128, 256, 16
NEG_INF = float("-inf")

VMEM_LIMIT = 56 * 1024 * 1024


def _cparams(sem):
    return pltpu.CompilerParams(dimension_semantics=sem, vmem_limit_bytes=VMEM_LIMIT)


def _dot(a, b):
    return jnp.dot(a, b, preferred_element_type=F32)


def _dot_nt(a, b):
    return lax.dot_general(a, b, (((1,), (1,)), ((), ())), preferred_element_type=F32)


def _dot_tn(a, b):
    return lax.dot_general(a, b, (((0,), (0,)), ((), ())), preferred_element_type=F32)


def _inproj_kernel(x_ref, g_ref, wm_ref, wl_ref, wg_ref, qkvo_ref, lat_ref, gate_ref):
    x = x_ref[...]
    ms = jnp.mean(x * x, axis=-1, keepdims=True)
    h = (x * lax.rsqrt(ms + RMS_EPS) * g_ref[...]).astype(BF16)
    main = _dot(h, wm_ref[...])
    kcol = lax.broadcasted_iota(jnp.int32, (1, main.shape[1]), 1) // (M_H * M_DK) == 1
    main = jnp.where(kcol, main * (M_DK ** -0.5), main)
    qkvo_ref[...] = main.astype(BF16)
    lat_ref[...] = _dot(h, wl_ref[...])
    gate_ref[...] = _dot(h, wg_ref[...])


def _in_proj(x2, g, w_main, w_lat, w_gate, tm):
    T = x2.shape[0]
    n_main, n_lat, n_gate = w_main.shape[1], w_lat.shape[1], w_gate.shape[1]
    full = lambda shape: pl.BlockSpec(shape, lambda i: (0, 0))
    return pl.pallas_call(
        _inproj_kernel,
        grid=(T // tm,),
        in_specs=[pl.BlockSpec((tm, D_MODEL), lambda i: (i, 0)), full((1, D_MODEL)),
                  full(w_main.shape), full(w_lat.shape), full(w_gate.shape)],
        out_specs=[pl.BlockSpec((tm, n_main), lambda i: (i, 0)),
                   pl.BlockSpec((tm, n_lat), lambda i: (i, 0)),
                   pl.BlockSpec((tm, n_gate), lambda i: (i, 0))],
        out_shape=[jax.ShapeDtypeStruct((T, n_main), BF16),
                   jax.ShapeDtypeStruct((T, n_lat), F32),
                   jax.ShapeDtypeStruct((T, n_gate), F32)],
        compiler_params=_cparams(("parallel",)),
        name="in_proj",
    )(x2, g, w_main, w_lat, w_gate)


def _log_sigmoid(x):
    return -(jnp.maximum(-x, 0.0) + jnp.log1p(jnp.exp(-jnp.abs(x))))


def _mlstm_kernel(qkvo_ref, gate_ref, bias_ref, ng_ref, y_ref, c_ref, m_ref, *, L, nbatch):
    c_idx = pl.program_id(1)

    @pl.when(c_idx == 0)
    def _():
        c_ref[...] = jnp.zeros_like(c_ref)
        m_ref[...] = jnp.zeros_like(m_ref)

    H, DK, DV = M_H, M_DK, M_DV
    row = lax.broadcasted_iota(jnp.int32, (L, L), 0)
    col = lax.broadcasted_iota(jnp.int32, (L, L), 1)
    causal = col <= row
    tri = jnp.where(causal, 1.0, 0.0)
    lane = lax.broadcasted_iota(jnp.int32, (L, 128), 1)
    ones_col = jnp.where(lane == 0, 1.0, 0.0).astype(BF16)

    for bb in range(nbatch):
        G = gate_ref[bb] + bias_ref[...]
        LF = _log_sigmoid(G)
        Bc = jnp.dot(tri, LF, preferred_element_type=F32, precision=lax.Precision.HIGHEST)
        ZT = jnp.where(lane < H, G, Bc).T

        for h in range(H):
            q = qkvo_ref[bb, :, h * DK:(h + 1) * DK]
            k = qkvo_ref[bb, :, H * DK + h * DK: H * DK + (h + 1) * DK]
            v = qkvo_ref[bb, :, 2 * H * DK + h * DV: 2 * H * DK + (h + 1) * DV]
            o = qkvo_ref[bb, :, 3 * H * DK + h * DV: 3 * H * DK + (h + 1) * DV]
            i_col = G[:, h:h + 1]
            b_col = Bc[:, H + h:H + h + 1]
            i_row = ZT[h:h + 1, :]
            b_row = ZT[H + h:H + h + 1, :]
            m_prev = m_ref[bb * H + h]
            c_prev = c_ref[bb * H + h]

            a_col = b_col + m_prev
            D = jnp.where(causal, b_col - b_row + i_row, NEG_INF)
            m_t = jnp.maximum(a_col, jnp.max(D, axis=-1, keepdims=True))
            Dw = jnp.exp(D - m_t)
            aw = jnp.exp(a_col - m_t)
            s = (_dot_nt(q, k) * Dw).astype(BF16)
            v_aug = jnp.concatenate([v, ones_col], axis=-1)
            num = aw * _dot(q, c_prev.astype(BF16)) + _dot(s, v_aug)
            den = num[:, DV:DV + 1]
            hval = num[:, :DV] / jnp.maximum(jnp.abs(den), jnp.exp(-m_t))

            bL = b_col[L - 1:L, :]
            g_col = bL - b_col + i_col
            m_new = jnp.maximum(bL + m_prev, jnp.max(g_col, axis=0, keepdims=True))
            decay = jnp.exp(bL + m_prev - m_new)
            w_col = jnp.exp(g_col - m_new)
            wv = (w_col * v_aug.astype(F32)).astype(BF16)
            c_ref[bb * H + h] = decay * c_prev + _dot_tn(k, wv)
            m_ref[bb * H + h] = m_new

            ms = jnp.mean(hval * hval, axis=-1, keepdims=True)
            hn = hval * lax.rsqrt(ms + RMS_EPS) * ng_ref[:, h * DV:(h + 1) * DV]
            y_ref[bb, :, h * DV:(h + 1) * DV] = (jax.nn.sigmoid(o.astype(F32)) * hn).astype(BF16)


def _mlstm(qkvo, gates, bias_row, norm_g, B, S, L, nbatch):
    nc = S // L
    n_in = qkvo.shape[1]
    y = pl.pallas_call(
        functools.partial(_mlstm_kernel, L=L, nbatch=nbatch),
        grid=(B // nbatch, nc),
        in_specs=[pl.BlockSpec((nbatch, L, n_in), lambda b, c: (b, c, 0)),
                  pl.BlockSpec((nbatch, L, 128), lambda b, c: (b, c, 0)),
                  pl.BlockSpec((1, 128), lambda b, c: (0, 0)),
                  pl.BlockSpec((1, M_H * M_DV), lambda b, c: (0, 0))],
        out_specs=pl.BlockSpec((nbatch, L, M_H * M_DV), lambda b, c: (b, c, 0)),
        out_shape=jax.ShapeDtypeStruct((B, S, M_H * M_DV), BF16),
        scratch_shapes=[pltpu.VMEM((nbatch * M_H, M_DK, 2 * M_DV), F32),
                        pltpu.VMEM((nbatch * M_H, 1, 1), F32)],
        compiler_params=_cparams(("parallel", "arbitrary")),
        name="mlstm",
    )(qkvo.reshape(B, S, n_in), gates.reshape(B, S, 128), bias_row, norm_g)
    return y.reshape(B * S, M_H * M_DV)


def _mla_prep_kernel(lat_ref, pos_ref, cqg_ref, ckvg_ref, wq_ref, wqs_ref, wkv_ref,
                     qg_ref, qgs_ref, kg_ref, kgs_ref, freq_ref, sign_ref,
                     q_ref, k_ref, v_ref):
    H = A_H
    cq = lat_ref[:, :A_QR]
    ckv = lat_ref[:, A_QR:A_QR + A_KVR]
    kr = lat_ref[:, A_QR + A_KVR:A_QR + A_KVR + 128]
    krs = lat_ref[:, A_QR + A_KVR + 128:A_QR + A_KVR + 256]

    def rms(t, g):
        return t * lax.rsqrt(jnp.mean(t * t, axis=-1, keepdims=True) + RMS_EPS) * g

    cqn = rms(cq, cqg_ref[...]).astype(BF16)
    ckvn = rms(ckv, ckvg_ref[...]).astype(BF16)
    qf = _dot(cqn, wq_ref[...])
    qs = _dot(cqn, wqs_ref[...])
    kv = _dot(ckvn, wkv_ref[...])

    ang = pos_ref[...] * freq_ref[...]
    cos = jnp.cos(ang)
    sin = jnp.sin(ang) * sign_ref[...]
    scale = A_DQK ** -0.5
    kr_ss = jnp.sum(kr * kr, axis=-1, keepdims=True)

    for h in range(H):
        qh = qf[:, h * A_DPAD:(h + 1) * A_DPAD]
        rstd = lax.rsqrt(jnp.sum(qh * qh, axis=-1, keepdims=True) / A_DQK + RMS_EPS) * scale
        q_nope = qh[:, :128] * rstd * qg_ref[:, :128]
        q_rope = qh[:, 128:] * rstd * qg_ref[:, 128:]
        q_rope_s = qs[:, h * 128:(h + 1) * 128] * rstd * qgs_ref[...]
        q_ref[:, h * A_DPAD:h * A_DPAD + 128] = q_nope.astype(BF16)
        q_ref[:, h * A_DPAD + 128:(h + 1) * A_DPAD] = (q_rope * cos + q_rope_s * sin).astype(BF16)

        kn = kv[:, h * 256:h * 256 + 128]
        rstd_k = lax.rsqrt((jnp.sum(kn * kn, axis=-1, keepdims=True) + kr_ss) / A_DQK + RMS_EPS)
        k_nope = kn * rstd_k * kg_ref[:, :128]
        k_rope = kr * rstd_k * kg_ref[:, 128:]
        k_rope_s = krs * rstd_k * kgs_ref[...]
        k_ref[:, h * A_DPAD:h * A_DPAD + 128] = k_nope.astype(BF16)
        k_ref[:, h * A_DPAD + 128:(h + 1) * A_DPAD] = (k_rope * cos + k_rope_s * sin).astype(BF16)
        v_ref[:, h * A_DV:(h + 1) * A_DV] = kv[:, h * 256 + 128:(h + 1) * 256].astype(BF16)


def _mla_prep(lat, pos_col, consts, tm):
    T = lat.shape[0]
    full = lambda a: pl.BlockSpec(a.shape, lambda i: (0, 0))
    return pl.pallas_call(
        _mla_prep_kernel,
        grid=(T // tm,),
        in_specs=[pl.BlockSpec((tm, lat.shape[1]), lambda i: (i, 0)),
                  pl.BlockSpec((tm, 1), lambda i: (i, 0))] + [full(c) for c in consts],
        out_specs=[pl.BlockSpec((tm, A_H * A_DPAD), lambda i: (i, 0)),
                   pl.BlockSpec((tm, A_H * A_DPAD), lambda i: (i, 0)),
                   pl.BlockSpec((tm, A_H * A_DV), lambda i: (i, 0))],
        out_shape=[jax.ShapeDtypeStruct((T, A_H * A_DPAD), BF16),
                   jax.ShapeDtypeStruct((T, A_H * A_DPAD), BF16),
                   jax.ShapeDtypeStruct((T, A_H * A_DV), BF16)],
        compiler_params=_cparams(("parallel",)),
        name="mla_prep",
    )(lat, pos_col, *consts)


def _attn_kernel(q_ref, k_ref, v_ref, o_ref, *, tq, hp):
    qi = pl.program_id(2)

    def chunk(j, carry, masked):
        rows = pl.ds(pl.multiple_of(j * tq, tq), tq)
        out = []
        for h in range(hp):
            m, l, acc = carry[h]
            q = q_ref[:, h * A_DPAD:(h + 1) * A_DPAD]
            k = k_ref[rows, h * A_DPAD:(h + 1) * A_DPAD]
            v = v_ref[rows, h * A_DV:(h + 1) * A_DV]
            s = _dot_nt(q, k)
            if masked:
                row = lax.broadcasted_iota(jnp.int32, (tq, tq), 0)
                col = lax.broadcasted_iota(jnp.int32, (tq, tq), 1)
                s = jnp.where(col <= row, s, NEG_INF)
            m_new = jnp.maximum(m, jnp.max(s, axis=-1, keepdims=True))
            alpha = jnp.exp(m - m_new)
            p = jnp.exp(s - m_new)
            l = alpha * l + jnp.sum(p, axis=-1, keepdims=True)
            acc = alpha * acc + _dot(p.astype(BF16), v)
            out.append((m_new, l, acc))
        return tuple(out)

    init = tuple((jnp.full((tq, 1), NEG_INF, F32), jnp.zeros((tq, 1), F32), jnp.zeros((tq, A_DV), F32))
                 for _ in range(hp))
    carry = lax.fori_loop(0, qi, lambda j, c: chunk(j, c, False), init)
    carry = chunk(qi, carry, True)
    for h in range(hp):
        m, l, acc = carry[h]
        o_ref[:, h * A_DV:(h + 1) * A_DV] = (acc / l).astype(BF16)


def _mla_attn(q, k, v, B, S, tq, hp):
    T = B * S
    nq = S // tq
    return pl.pallas_call(
        functools.partial(_attn_kernel, tq=tq, hp=hp),
        grid=(B, A_H // hp, nq),
        in_specs=[pl.BlockSpec((tq, hp * A_DPAD), lambda b, h, i: (b * nq + i, h)),
                  pl.BlockSpec((S, hp * A_DPAD), lambda b, h, i: (b, h)),
                  pl.BlockSpec((S, hp * A_DV), lambda b, h, i: (b, h))],
        out_specs=pl.BlockSpec((tq, hp * A_DV), lambda b, h, i: (b * nq + i, h)),
        out_shape=jax.ShapeDtypeStruct((T, A_H * A_DV), BF16),
        compiler_params=_cparams(("parallel", "parallel", "arbitrary")),
        name="mla_attn",
    )(q, k, v)


def _outproj_kernel(x_ref, ym_ref, ya_ref, wom_ref, woa_ref, g_ref, wq_ref, k1_ref, k2_ref,
                    x1_ref, h2_ref, s1_ref, s2_ref):
    x1 = x_ref[...] + _dot(ym_ref[...], wom_ref[...]) + _dot(ya_ref[...], woa_ref[...])
    x1_ref[...] = x1
    ms = jnp.mean(x1 * x1, axis=-1, keepdims=True)
    h2f = x1 * lax.rsqrt(ms + RMS_EPS) * g_ref[...]
    h2 = h2f.astype(BF16)
    h2_ref[...] = h2
    qry = _dot(h2, wq_ref[...]).astype(BF16)
    half = P_DK // 2
    for h in range(P_H):
        s1_ref[h] = _dot_nt(k1_ref[...], qry[:, h * P_DK:h * P_DK + half])
        s2_ref[h] = _dot_nt(k2_ref[...], qry[:, h * P_DK + half:(h + 1) * P_DK])


def _out_proj(x2, ym, ya, wo_m, wo_a, g, wq, k1, k2, tm):
    T = x2.shape[0]
    full = lambda a: pl.BlockSpec(a.shape, lambda i: (0, 0))
    return pl.pallas_call(
        _outproj_kernel,
        grid=(T // tm,),
        in_specs=[pl.BlockSpec((tm, D_MODEL), lambda i: (i, 0)),
                  pl.BlockSpec((tm, ym.shape[1]), lambda i: (i, 0)),
                  pl.BlockSpec((tm, ya.shape[1]), lambda i: (i, 0)),
                  full(wo_m), full(wo_a), full(g), full(wq), full(k1), full(k2)],
        out_specs=[pl.BlockSpec((tm, D_MODEL), lambda i: (i, 0)),
                   pl.BlockSpec((tm, D_MODEL), lambda i: (i, 0)),
                   pl.BlockSpec((P_H, P_NK, tm), lambda i: (0, 0, i)),
                   pl.BlockSpec((P_H, P_NK, tm), lambda i: (0, 0, i))],
        out_shape=[jax.ShapeDtypeStruct((T, D_MODEL), F32),
                   jax.ShapeDtypeStruct((T, D_MODEL), BF16),
                   jax.ShapeDtypeStruct((P_H, P_NK, T), F32),
                   jax.ShapeDtypeStruct((P_H, P_NK, T), F32)],
        compiler_params=_cparams(("parallel",)),
        name="out_peerq",
    )(x2, ym, ya, wo_m, wo_a, g, wq, k1, k2)


def _top16(x):
    vals = []
    for _ in range(P_TOPK):
        m = jnp.max(x, axis=0, keepdims=True)
        vals.append(m)
        x = jnp.where(x == m, NEG_INF, x)
    return vals


def _oddeven_merge(lo, hi, r):
    step = r * 2
    if step < hi - lo:
        yield from _oddeven_merge(lo, hi, step)
        yield from _oddeven_merge(lo + r, hi, step)
        yield from [(i, i + r) for i in range(lo + r, hi - r, step)]
    else:
        yield (lo, lo + r)


def _oddeven_merge_sort(lo, hi):
    if hi - lo >= 1:
        mid = lo + (hi - lo) // 2
        yield from _oddeven_merge_sort(lo, mid)
        yield from _oddeven_merge_sort(mid + 1, hi)
        yield from _oddeven_merge(lo, hi, 1)


_SORT16 = tuple(_oddeven_merge_sort(0, P_TOPK - 1))
_SUBLANES = 8


def _cmpx(v, i, j):
    a, b = v[i], v[j]
    v[i] = jnp.maximum(a, b)
    v[j] = jnp.minimum(a, b)


def _top16_sorted(x):
    v = [x[i * _SUBLANES:(i + 1) * _SUBLANES, :] for i in range(P_TOPK)]
    for i, j in _SORT16:
        _cmpx(v, i, j)
    for shift in (4, 2, 1):
        o = [pltpu.roll(t, shift, axis=0) for t in v]
        v = [jnp.maximum(v[i], o[P_TOPK - 1 - i]) for i in range(P_TOPK)]
        for d in (8, 4, 2, 1):
            for i in range(P_TOPK):
                if not i & d:
                    _cmpx(v, i, i + d)
    return v


_CAND = [(a, b) for a in range(P_TOPK) for b in range(P_TOPK) if (a + 1) * (b + 1) <= P_TOPK]


def _route_kernel(s1_ref, s2_ref, r2_ref, e2_ref, cnt_ref, f_ref):
    s1 = s1_ref[0]
    s2 = s2_ref[0]
    ng = P_NK // _SUBLANES
    v1s = _top16_sorted(s1)
    v2s = _top16_sorted(s2)
    v1 = [t[0:1, :] for t in v1s]
    v2 = [t[0:1, :] for t in v2s]
    cand = jnp.concatenate([v1[a] + v2[b] for a, b in _CAND], axis=0)
    tau = _top16(cand)[P_TOPK - 1]
    top = v1[0] + v2[0]
    z = jnp.zeros_like(tau)
    cnt_a = [jnp.zeros_like(tau) for _ in range(P_TOPK)]
    for a, b in _CAND:
        sm = v1[a] + v2[b]
        sel = sm >= tau
        z = z + jnp.where(sel, jnp.exp(sm - top), 0.0)
        cnt_a[a] = cnt_a[a] + jnp.where(sel, 1.0, 0.0)
    cnt_b = [jnp.broadcast_to(c, (_SUBLANES, c.shape[1])) for c in cnt_a]
    for i in range(ng):
        rows = slice(i * _SUBLANES, (i + 1) * _SUBLANES)
        x1, x2 = s1[rows, :], s2[rows, :]
        r2 = jnp.full_like(x2, float(P_TOPK))
        cnt = jnp.zeros_like(x1)
        for a in reversed(range(P_TOPK)):
            r2 = jnp.where(x2 >= v2s[a], float(a), r2)
            cnt = jnp.where(x1 == v1s[a], cnt_b[a], cnt)
        r2_ref[0, rows, :] = r2
        cnt_ref[0, rows, :] = cnt
    e2_ref[0] = jnp.exp(s2 - v2[0])
    f_ref[0] = jnp.exp(s1 - v1[0]) / z


def _peer_route(s1t, s2t, tl):
    T = s1t.shape[2]
    spec = pl.BlockSpec((1, P_NK, tl), lambda h, i: (h, 0, i))
    shp32 = jax.ShapeDtypeStruct((P_H, P_NK, T), F32)
    return pl.pallas_call(
        _route_kernel,
        grid=(P_H, T // tl),
        in_specs=[spec, spec],
        out_specs=[spec, spec, spec, spec],
        out_shape=[shp32, shp32, shp32, shp32],
        compiler_params=_cparams(("parallel", "parallel")),
        name="peer_route",
    )(s1t, s2t)


def _gelu2(x):
    return x * (1.0 + lax.erf(x * (2.0 ** -0.5)))


_LANES = 128
_PACK = 16
_N1_GROUP = 1


def _peer_kernel(h2_ref, x1_ref, u_ref, v_ref, r2f_ref, e2f_ref, cnt_ref, f_ref, o_ref,
                 acc_ref, a_ref, p_ref, r2_ref, e2_ref, *, nb, tm):
    e = pl.program_id(1)

    @pl.when(e == 0)
    def _():
        acc_ref[...] = jnp.zeros_like(acc_ref)
        for h in range(P_H):
            r2_ref[h] = r2f_ref[h].astype(BF16)
            e2_ref[h] = e2f_ref[h].astype(BF16)

    a_ref[...] = _dot_nt(u_ref[...], h2_ref[...])
    n1_0 = pl.multiple_of(e * nb, nb)
    zero = jnp.zeros((_PACK, _LANES), BF16)
    for g in range(tm // _LANES):
        lanes = slice(g * _LANES, (g + 1) * _LANES)
        for jc in range(nb // _N1_GROUP):
            w = [[None] * (P_NK // _PACK) for _ in range(_N1_GROUP)]
            for h in range(P_H):
                cnt8 = cnt_ref[h, pl.ds(n1_0, nb), lanes]
                f8 = 0.5 * f_ref[h, pl.ds(n1_0, nb), lanes]
                r2t = [r2_ref[h, r * _PACK:(r + 1) * _PACK, lanes] for r in range(P_NK // _PACK)]
                e2t = [e2_ref[h, r * _PACK:(r + 1) * _PACK, lanes] for r in range(P_NK // _PACK)]
                for jj in range(_N1_GROUP):
                    j = jc * _N1_GROUP + jj
                    cb = jnp.broadcast_to(cnt8[j:j + 1, :], (_PACK, _LANES)).astype(BF16)
                    fb = jnp.broadcast_to(f8[j:j + 1, :], (_PACK, _LANES)).astype(BF16)
                    for r in range(P_NK // _PACK):
                        t = jnp.where(r2t[r] < cb, e2t[r], zero) * fb
                        w[jj][r] = t if w[jj][r] is None else w[jj][r] + t
            for jj in range(_N1_GROUP):
                j = jc * _N1_GROUP + jj
                for r in range(P_NK // _PACK):
                    rows = slice(j * P_NK + r * _PACK, j * P_NK + (r + 1) * _PACK)
                    p_ref[rows, lanes] = w[jj][r] * _gelu2(a_ref[rows, lanes]).astype(BF16)
    acc_ref[...] += _dot_tn(p_ref[...], v_ref[...])

    @pl.when(e == pl.num_programs(1) - 1)
    def _():
        o_ref[...] = x1_ref[...] + acc_ref[...]


def _peer_mlp(h2, x1, u, v, r2t, e2t, cntt, ft, tm, nb):
    T = x1.shape[0]
    ne = P_NK // nb
    te = nb * P_NK
    rspec = pl.BlockSpec((P_H, P_NK, tm), lambda i, e: (0, 0, i))
    return pl.pallas_call(
        functools.partial(_peer_kernel, nb=nb, tm=tm),
        grid=(T // tm, ne),
        in_specs=[pl.BlockSpec((tm, D_MODEL), lambda i, e: (i, 0)),
                  pl.BlockSpec((tm, D_MODEL), lambda i, e: (i, 0)),
                  pl.BlockSpec((te, D_MODEL), lambda i, e: (e, 0)),
                  pl.BlockSpec((te, D_MODEL), lambda i, e: (e, 0)),
                  rspec, rspec, rspec, rspec],
        out_specs=pl.BlockSpec((tm, D_MODEL), lambda i, e: (i, 0)),
        out_shape=jax.ShapeDtypeStruct((T, D_MODEL), F32),
        scratch_shapes=[pltpu.VMEM((tm, D_MODEL), F32),
                        pltpu.VMEM((te, tm), F32),
                        pltpu.VMEM((te, tm), BF16),
                        pltpu.VMEM((P_H, P_NK, tm), BF16),
                        pltpu.VMEM((P_H, P_NK, tm), BF16)],
        compiler_params=_cparams(("parallel", "arbitrary")),
        name="peer_mlp",
    )(h2, x1, u, v, r2t, e2t, cntt, ft)


def _swap_halves(t, axis=-1):
    a, b = jnp.split(t, 2, axis=axis)
    return jnp.concatenate([b, a], axis=axis)


def _pad_cols(t, n):
    return jnp.pad(t, ((0, 0), (0, n - t.shape[1])))


def _layer(x2, pos_col, B, S, attn_norm_g, w_in, b_igate, b_fgate, mlstm_norm_g, cq_norm_g, w_uq,
           ckv_norm_g, w_ukv, q_norm_g, k_norm_g, w_out, ffn_norm_g, peer_w_q, peer_keys_1,
           peer_keys_2, peer_u, peer_v):
    T = B * S
    n_qkvo = 4 * M_H * M_DK
    o_gate = n_qkvo
    o_cq = o_gate + 2 * M_H
    o_ckv = o_cq + A_QR
    o_kr = o_ckv + A_KVR

    w_main = w_in[:, :n_qkvo].astype(BF16)
    w_kr = w_in[:, o_kr:o_kr + A_ROPE]
    w_lat = jnp.concatenate([w_in[:, o_cq:o_kr], _pad_cols(w_kr, 128),
                             _pad_cols(_swap_halves(w_kr), 128)], axis=1).astype(BF16)
    w_gate = _pad_cols(w_in[:, o_gate:o_gate + 2 * M_H], 128).astype(BF16)
    bias_row = _pad_cols(jnp.concatenate([b_igate, b_fgate])[None, :], 128)

    wq_h = w_uq.reshape(A_QR, A_H, A_DQK)
    wq_pad = jnp.pad(wq_h, ((0, 0), (0, 0), (0, A_DPAD - A_DQK))).reshape(A_QR, A_H * A_DPAD).astype(BF16)
    wq_sw = jnp.pad(_swap_halves(wq_h[:, :, A_NOPE:]), ((0, 0), (0, 0), (0, 128 - A_ROPE)))
    wq_sw = wq_sw.reshape(A_QR, A_H * 128).astype(BF16)
    w_kv = w_ukv.astype(BF16)
    qg = _pad_cols(q_norm_g[None, :], A_DPAD)
    qgs = _pad_cols(_swap_halves(q_norm_g[A_NOPE:])[None, :], 128)
    kg = _pad_cols(k_norm_g[None, :], A_DPAD)
    kgs = _pad_cols(_swap_halves(k_norm_g[A_NOPE:])[None, :], 128)
    inv_freq = ROPE_THETA ** (-jnp.arange(0, A_ROPE, 2, dtype=F32) / A_ROPE)
    freq_row = _pad_cols(jnp.concatenate([inv_freq, inv_freq])[None, :], 128)
    half = A_ROPE // 2
    sign_row = _pad_cols(jnp.concatenate([-jnp.ones((half,), F32), jnp.ones((half,), F32)])[None, :], 128)

    qkvo, lat, gates = _in_proj(x2, attn_norm_g[None, :], w_main, w_lat, w_gate, tm=min(512, T))
    y_m = _mlstm(qkvo, gates, bias_row, mlstm_norm_g[None, :], B, S, L=min(256, S), nbatch=1)
    consts = [cq_norm_g[None, :], ckv_norm_g[None, :], wq_pad, wq_sw, w_kv, qg, qgs, kg, kgs,
              freq_row, sign_row]
    q, k, v = _mla_prep(lat, pos_col, consts, tm=min(512, T))
    y_a = _mla_attn(q, k, v, B, S, tq=min(1024, S), hp=2)
    wo = w_out.astype(BF16)
    x1, h2, s1t, s2t = _out_proj(x2, y_m, y_a, wo[:M_H * M_DV], wo[M_H * M_DV:], ffn_norm_g[None, :],
                                  peer_w_q.astype(BF16), peer_keys_1.astype(BF16),
                                  peer_keys_2.astype(BF16), tm=min(512, T))
    r2t, e2t, cntt, ft = _peer_route(s1t, s2t, tl=min(256, T))
    return _peer_mlp(h2, x1, peer_u.astype(BF16), peer_v.astype(BF16), r2t, e2t, cntt, ft,
                     tm=min(512, T), nb=8)


def kernel(x, positions, attn_norm_g, w_in, b_igate, b_fgate, mlstm_norm_g, cq_norm_g, w_uq, ckv_norm_g,
           w_ukv, q_norm_g, k_norm_g, w_out, ffn_norm_g, peer_w_q, peer_keys_1, peer_keys_2, peer_u, peer_v):
    B, S, D = x.shape
    x2 = x.reshape(B * S, D)
    pos_col = positions.reshape(B * S, 1).astype(F32)
    for l in range(attn_norm_g.shape[0]):
        x2 = _layer(x2, pos_col, B, S, attn_norm_g[l], w_in[l], b_igate[l], b_fgate[l], mlstm_norm_g[l],
                    cq_norm_g[l], w_uq[l], ckv_norm_g[l], w_ukv[l], q_norm_g[l], k_norm_g[l], w_out[l],
                    ffn_norm_g[l], peer_w_q[l], peer_keys_1[l], peer_keys_2[l], peer_u[l], peer_v[l])
    return x2.reshape(B, S, D)
```

```python
import functools

import jax
import jax.numpy as jnp
from jax import lax
from jax.experimental import pallas as pl
from jax.experimental.pallas import tpu as pltpu

F32 = jnp.float32
BF16 = jnp.bfloat16

D_MODEL = 1024
RMS_EPS = 1e-6
M_H, M_DK, M_DV = 4, 128, 128
A_H, A_QR, A_KVR, A_NOPE, A_ROPE, A_DV = 4, 256, 128, 128, 64, 128
A_DQK = A_NOPE + A_ROPE
A_DPAD = 256
ROPE_THETA = 10000.0
P_H, P_NK, P_DK, P_TOPK = 8, 128, 256, 16
NEG_INF = float("-inf")

VMEM_LIMIT = 56 * 1024 * 1024


def _cparams(sem):
    return pltpu.CompilerParams(dimension_semantics=sem, vmem_limit_bytes=VMEM_LIMIT)


def _dot(a, b):
    return jnp.dot(a, b, preferred_element_type=F32)


def _dot_nt(a, b):
    return lax.dot_general(a, b, (((1,), (1,)), ((), ())), preferred_element_type=F32)


def _dot_tn(a, b):
    return lax.dot_general(a, b, (((0,), (0,)), ((), ())), preferred_element_type=F32)


def _inproj_kernel(x_ref, g_ref, wm_ref, wl_ref, wg_ref, qkvo_ref, lat_ref, gate_ref):
    x = x_ref[...]
    ms = jnp.mean(x * x, axis=-1, keepdims=True)
    h = (x * lax.rsqrt(ms + RMS_EPS) * g_ref[...]).astype(BF16)
    main = _dot(h, wm_ref[...])
    kcol = lax.broadcasted_iota(jnp.int32, (1, main.shape[1]), 1) // (M_H * M_DK) == 1
    main = jnp.where(kcol, main * (M_DK ** -0.5), main)
    qkvo_ref[...] = main.astype(BF16)
    lat_ref[...] = _dot(h, wl_ref[...])
    gate_ref[...] = _dot(h, wg_ref[...])


def _in_proj(x2, g, w_main, w_lat, w_gate, tm):
    T = x2.shape[0]
    n_main, n_lat, n_gate = w_main.shape[1], w_lat.shape[1], w_gate.shape[1]
    full = lambda shape: pl.BlockSpec(shape, lambda i: (0, 0))
    return pl.pallas_call(
        _inproj_kernel,
        grid=(T // tm,),
        in_specs=[pl.BlockSpec((tm, D_MODEL), lambda i: (i, 0)), full((1, D_MODEL)),
                  full(w_main.shape), full(w_lat.shape), full(w_gate.shape)],
        out_specs=[pl.BlockSpec((tm, n_main), lambda i: (i, 0)),
                   pl.BlockSpec((tm, n_lat), lambda i: (i, 0)),
                   pl.BlockSpec((tm, n_gate), lambda i: (i, 0))],
        out_shape=[jax.ShapeDtypeStruct((T, n_main), BF16),
                   jax.ShapeDtypeStruct((T, n_lat), F32),
                   jax.ShapeDtypeStruct((T, n_gate), F32)],
        compiler_params=_cparams(("parallel",)),
        name="in_proj",
    )(x2, g, w_main, w_lat, w_gate)


def _log_sigmoid(x):
    return -(jnp.maximum(-x, 0.0) + jnp.log1p(jnp.exp(-jnp.abs(x))))


def _mlstm_kernel(qkvo_ref, gate_ref, bias_ref, ng_ref, y_ref, c_ref, m_ref, *, L, nbatch):
    c_idx = pl.program_id(1)

    @pl.when(c_idx == 0)
    def _():
        c_ref[...] = jnp.zeros_like(c_ref)
        m_ref[...] = jnp.zeros_like(m_ref)

    H, DK, DV = M_H, M_DK, M_DV
    row = lax.broadcasted_iota(jnp.int32, (L, L), 0)
    col = lax.broadcasted_iota(jnp.int32, (L, L), 1)
    causal = col <= row
    tri = jnp.where(causal, 1.0, 0.0)
    lane = lax.broadcasted_iota(jnp.int32, (L, 128), 1)
    ones_col = jnp.where(lane == 0, 1.0, 0.0).astype(BF16)

    for bb in range(nbatch):
        G = gate_ref[bb] + bias_ref[...]
        LF = _log_sigmoid(G)
        Bc = jnp.dot(tri, LF, preferred_element_type=F32, precision=lax.Precision.HIGHEST)
        ZT = jnp.where(lane < H, G, Bc).T

        for h in range(H):
            q = qkvo_ref[bb, :, h * DK:(h + 1) * DK]
            k = qkvo_ref[bb, :, H * DK + h * DK: H * DK + (h + 1) * DK]
            v = qkvo_ref[bb, :, 2 * H * DK + h * DV: 2 * H * DK + (h + 1) * DV]
            o = qkvo_ref[bb, :, 3 * H * DK + h * DV: 3 * H * DK + (h + 1) * DV]
            i_col = G[:, h:h + 1]
            b_col = Bc[:, H + h:H + h + 1]
            i_row = ZT[h:h + 1, :]
            b_row = ZT[H + h:H + h + 1, :]
            m_prev = m_ref[bb * H + h]
            c_prev = c_ref[bb * H + h]

            a_col = b_col + m_prev
            D = jnp.where(causal, b_col - b_row + i_row, NEG_INF)
            m_t = jnp.maximum(a_col, jnp.max(D, axis=-1, keepdims=True))
            Dw = jnp.exp(D - m_t)
            aw = jnp.exp(a_col - m_t)
            s = (_dot_nt(q, k) * Dw).astype(BF16)
            v_aug = jnp.concatenate([v, ones_col], axis=-1)
            num = aw * _dot(q, c_prev.astype(BF16)) + _dot(s, v_aug)
            den = num[:, DV:DV + 1]
            hval = num[:, :DV] / jnp.maximum(jnp.abs(den), jnp.exp(-m_t))

            bL = b_col[L - 1:L, :]
            g_col = bL - b_col + i_col
            m_new = jnp.maximum(bL + m_prev, jnp.max(g_col, axis=0, keepdims=True))
            decay = jnp.exp(bL + m_prev - m_new)
            w_col = jnp.exp(g_col - m_new)
            wv = (w_col * v_aug.astype(F32)).astype(BF16)
            c_ref[bb * H + h] = decay * c_prev + _dot_tn(k, wv)
            m_ref[bb * H + h] = m_new

            ms = jnp.mean(hval * hval, axis=-1, keepdims=True)
            hn = hval * lax.rsqrt(ms + RMS_EPS) * ng_ref[:, h * DV:(h + 1) * DV]
            y_ref[bb, :, h * DV:(h + 1) * DV] = (jax.nn.sigmoid(o.astype(F32)) * hn).astype(BF16)


def _mlstm(qkvo, gates, bias_row, norm_g, B, S, L, nbatch):
    nc = S // L
    n_in = qkvo.shape[1]
    y = pl.pallas_call(
        functools.partial(_mlstm_kernel, L=L, nbatch=nbatch),
        grid=(B // nbatch, nc),
        in_specs=[pl.BlockSpec((nbatch, L, n_in), lambda b, c: (b, c, 0)),
                  pl.BlockSpec((nbatch, L, 128), lambda b, c: (b, c, 0)),
                  pl.BlockSpec((1, 128), lambda b, c: (0, 0)),
                  pl.BlockSpec((1, M_H * M_DV), lambda b, c: (0, 0))],
        out_specs=pl.BlockSpec((nbatch, L, M_H * M_DV), lambda b, c: (b, c, 0)),
        out_shape=jax.ShapeDtypeStruct((B, S, M_H * M_DV), BF16),
        scratch_shapes=[pltpu.VMEM((nbatch * M_H, M_DK, 2 * M_DV), F32),
                        pltpu.VMEM((nbatch * M_H, 1, 1), F32)],
        compiler_params=_cparams(("parallel", "arbitrary")),
        name="mlstm",
    )(qkvo.reshape(B, S, n_in), gates.reshape(B, S, 128), bias_row, norm_g)
    return y.reshape(B * S, M_H * M_DV)


def _mla_prep_kernel(lat_ref, pos_ref, cqg_ref, ckvg_ref, wq_ref, wqs_ref, wkv_ref,
                     qg_ref, qgs_ref, kg_ref, kgs_ref, freq_ref, sign_ref,
                     q_ref, k_ref, v_ref):
    H = A_H
    cq = lat_ref[:, :A_QR]
    ckv = lat_ref[:, A_QR:A_QR + A_KVR]
    kr = lat_ref[:, A_QR + A_KVR:A_QR + A_KVR + 128]
    krs = lat_ref[:, A_QR + A_KVR + 128:A_QR + A_KVR + 256]

    def rms(t, g):
        return t * lax.rsqrt(jnp.mean(t * t, axis=-1, keepdims=True) + RMS_EPS) * g

    cqn = rms(cq, cqg_ref[...]).astype(BF16)
    ckvn = rms(ckv, ckvg_ref[...]).astype(BF16)
    qf = _dot(cqn, wq_ref[...])
    qs = _dot(cqn, wqs_ref[...])
    kv = _dot(ckvn, wkv_ref[...])

    ang = pos_ref[...] * freq_ref[...]
    cos = jnp.cos(ang)
    sin = jnp.sin(ang) * sign_ref[...]
    scale = A_DQK ** -0.5
    kr_ss = jnp.sum(kr * kr, axis=-1, keepdims=True)

    for h in range(H):
        qh = qf[:, h * A_DPAD:(h + 1) * A_DPAD]
        rstd = lax.rsqrt(jnp.sum(qh * qh, axis=-1, keepdims=True) / A_DQK + RMS_EPS) * scale
        q_nope = qh[:, :128] * rstd * qg_ref[:, :128]
        q_rope = qh[:, 128:] * rstd * qg_ref[:, 128:]
        q_rope_s = qs[:, h * 128:(h + 1) * 128] * rstd * qgs_ref[...]
        q_ref[:, h * A_DPAD:h * A_DPAD + 128] = q_nope.astype(BF16)
        q_ref[:, h * A_DPAD + 128:(h + 1) * A_DPAD] = (q_rope * cos + q_rope_s * sin).astype(BF16)

        kn = kv[:, h * 256:h * 256 + 128]
        rstd_k = lax.rsqrt((jnp.sum(kn * kn, axis=-1, keepdims=True) + kr_ss) / A_DQK + RMS_EPS)
        k_nope = kn * rstd_k * kg_ref[:, :128]
        k_rope = kr * rstd_k * kg_ref[:, 128:]
        k_rope_s = krs * rstd_k * kgs_ref[...]
        k_ref[:, h * A_DPAD:h * A_DPAD + 128] = k_nope.astype(BF16)
        k_ref[:, h * A_DPAD + 128:(h + 1) * A_DPAD] = (k_rope * cos + k_rope_s * sin).astype(BF16)
        v_ref[:, h * A_DV:(h + 1) * A_DV] = kv[:, h * 256 + 128:(h + 1) * 256].astype(BF16)


def _mla_prep(lat, pos_col, consts, tm):
    T = lat.shape[0]
    full = lambda a: pl.BlockSpec(a.shape, lambda i: (0, 0))
    return pl.pallas_call(
        _mla_prep_kernel,
        grid=(T // tm,),
        in_specs=[pl.BlockSpec((tm, lat.shape[1]), lambda i: (i, 0)),
                  pl.BlockSpec((tm, 1), lambda i: (i, 0))] + [full(c) for c in consts],
        out_specs=[pl.BlockSpec((tm, A_H * A_DPAD), lambda i: (i, 0)),
                   pl.BlockSpec((tm, A_H * A_DPAD), lambda i: (i, 0)),
                   pl.BlockSpec((tm, A_H * A_DV), lambda i: (i, 0))],
        out_shape=[jax.ShapeDtypeStruct((T, A_H * A_DPAD), BF16),
                   jax.ShapeDtypeStruct((T, A_H * A_DPAD), BF16),
                   jax.ShapeDtypeStruct((T, A_H * A_DV), BF16)],
        compiler_params=_cparams(("parallel",)),
        name="mla_prep",
    )(lat, pos_col, *consts)


def _attn_kernel(q_ref, k_ref, v_ref, o_ref, *, tq, hp):
    qi = pl.program_id(2)

    def chunk(j, carry, masked):
        rows = pl.ds(pl.multiple_of(j * tq, tq), tq)
        out = []
        for h in range(hp):
            m, l, acc = carry[h]
            q = q_ref[:, h * A_DPAD:(h + 1) * A_DPAD]
            k = k_ref[rows, h * A_DPAD:(h + 1) * A_DPAD]
            v = v_ref[rows, h * A_DV:(h + 1) * A_DV]
            s = _dot_nt(q, k)
            if masked:
                row = lax.broadcasted_iota(jnp.int32, (tq, tq), 0)
                col = lax.broadcasted_iota(jnp.int32, (tq, tq), 1)
                s = jnp.where(col <= row, s, NEG_INF)
            m_new = jnp.maximum(m, jnp.max(s, axis=-1, keepdims=True))
            alpha = jnp.exp(m - m_new)
            p = jnp.exp(s - m_new)
            l = alpha * l + jnp.sum(p, axis=-1, keepdims=True)
            acc = alpha * acc + _dot(p.astype(BF16), v)
            out.append((m_new, l, acc))
        return tuple(out)

    init = tuple((jnp.full((tq, 1), NEG_INF, F32), jnp.zeros((tq, 1), F32), jnp.zeros((tq, A_DV), F32))
                 for _ in range(hp))
    carry = lax.fori_loop(0, qi, lambda j, c: chunk(j, c, False), init)
    carry = chunk(qi, carry, True)
    for h in range(hp):
        m, l, acc = carry[h]
        o_ref[:, h * A_DV:(h + 1) * A_DV] = (acc / l).astype(BF16)


def _mla_attn(q, k, v, B, S, tq, hp):
    T = B * S
    nq = S // tq
    return pl.pallas_call(
        functools.partial(_attn_kernel, tq=tq, hp=hp),
        grid=(B, A_H // hp, nq),
        in_specs=[pl.BlockSpec((tq, hp * A_DPAD), lambda b, h, i: (b * nq + i, h)),
                  pl.BlockSpec((S, hp * A_DPAD), lambda b, h, i: (b, h)),
                  pl.BlockSpec((S, hp * A_DV), lambda b, h, i: (b, h))],
        out_specs=pl.BlockSpec((tq, hp * A_DV), lambda b, h, i: (b * nq + i, h)),
        out_shape=jax.ShapeDtypeStruct((T, A_H * A_DV), BF16),
        compiler_params=_cparams(("parallel", "parallel", "arbitrary")),
        name="mla_attn",
    )(q, k, v)


def _outproj_kernel(x_ref, ym_ref, ya_ref, wom_ref, woa_ref, g_ref, wq_ref, k1_ref, k2_ref,
                    x1_ref, h2_ref, s1_ref, s2_ref):
    x1 = x_ref[...] + _dot(ym_ref[...], wom_ref[...]) + _dot(ya_ref[...], woa_ref[...])
    x1_ref[...] = x1
    ms = jnp.mean(x1 * x1, axis=-1, keepdims=True)
    h2f = x1 * lax.rsqrt(ms + RMS_EPS) * g_ref[...]
    h2 = h2f.astype(BF16)
    h2_ref[...] = h2
    qry = _dot(h2, wq_ref[...]).astype(BF16)
    half = P_DK // 2
    for h in range(P_H):
        s1_ref[h] = _dot_nt(k1_ref[...], qry[:, h * P_DK:h * P_DK + half])
        s2_ref[h] = _dot_nt(k2_ref[...], qry[:, h * P_DK + half:(h + 1) * P_DK])


def _out_proj(x2, ym, ya, wo_m, wo_a, g, wq, k1, k2, tm):
    T = x2.shape[0]
    full = lambda a: pl.BlockSpec(a.shape, lambda i: (0, 0))
    return pl.pallas_call(
        _outproj_kernel,
        grid=(T // tm,),
        in_specs=[pl.BlockSpec((tm, D_MODEL), lambda i: (i, 0)),
                  pl.BlockSpec((tm, ym.shape[1]), lambda i: (i, 0)),
                  pl.BlockSpec((tm, ya.shape[1]), lambda i: (i, 0)),
                  full(wo_m), full(wo_a), full(g), full(wq), full(k1), full(k2)],
        out_specs=[pl.BlockSpec((tm, D_MODEL), lambda i: (i, 0)),
                   pl.BlockSpec((tm, D_MODEL), lambda i: (i, 0)),
                   pl.BlockSpec((P_H, P_NK, tm), lambda i: (0, 0, i)),
                   pl.BlockSpec((P_H, P_NK, tm), lambda i: (0, 0, i))],
        out_shape=[jax.ShapeDtypeStruct((T, D_MODEL), F32),
                   jax.ShapeDtypeStruct((T, D_MODEL), BF16),
                   jax.ShapeDtypeStruct((P_H, P_NK, T), F32),
                   jax.ShapeDtypeStruct((P_H, P_NK, T), F32)],
        compiler_params=_cparams(("parallel",)),
        name="out_peerq",
    )(x2, ym, ya, wo_m, wo_a, g, wq, k1, k2)


def _top16(x):
    vals = []
    for _ in range(P_TOPK):
        m = jnp.max(x, axis=0, keepdims=True)
        vals.append(m)
        x = jnp.where(x == m, NEG_INF, x)
    return vals


def _oddeven_merge(lo, hi, r):
    step = r * 2
    if step < hi - lo:
        yield from _oddeven_merge(lo, hi, step)
        yield from _oddeven_merge(lo + r, hi, step)
        yield from [(i, i + r) for i in range(lo + r, hi - r, step)]
    else:
        yield (lo, lo + r)


def _oddeven_merge_sort(lo, hi):
    if hi - lo >= 1:
        mid = lo + (hi - lo) // 2
        yield from _oddeven_merge_sort(lo, mid)
        yield from _oddeven_merge_sort(mid + 1, hi)
        yield from _oddeven_merge(lo, hi, 1)


_SORT16 = tuple(_oddeven_merge_sort(0, P_TOPK - 1))
_SUBLANES = 8


def _cmpx(v, i, j):
    a, b = v[i], v[j]
    v[i] = jnp.maximum(a, b)
    v[j] = jnp.minimum(a, b)


def _top16_sorted(x):
    v = [x[i * _SUBLANES:(i + 1) * _SUBLANES, :] for i in range(P_TOPK)]
    for i, j in _SORT16:
        _cmpx(v, i, j)
    for shift in (4, 2, 1):
        o = [pltpu.roll(t, shift, axis=0) for t in v]
        v = [jnp.maximum(v[i], o[P_TOPK - 1 - i]) for i in range(P_TOPK)]
        for d in (8, 4, 2, 1):
            for i in range(P_TOPK):
                if not i & d:
                    _cmpx(v, i, i + d)
    return v


_CAND = [(a, b) for a in range(P_TOPK) for b in range(P_TOPK) if (a + 1) * (b + 1) <= P_TOPK]


_ROUTE_LANES = 256


def _route_kernel(s1_ref, s2_ref, r2_ref, e2_ref, cnt_ref, f_ref):
    ng = P_NK // _SUBLANES
    for c in range(s1_ref.shape[2] // _ROUTE_LANES):
        lanes = slice(c * _ROUTE_LANES, (c + 1) * _ROUTE_LANES)
        s1 = s1_ref[0, :, lanes]
        s2 = s2_ref[0, :, lanes]
        v1s = _top16_sorted(s1)
        v2s = _top16_sorted(s2)
        v1 = [t[0:1, :] for t in v1s]
        v2 = [t[0:1, :] for t in v2s]
        cand = jnp.concatenate([v1[a] + v2[b] for a, b in _CAND], axis=0)
        tau = _top16(cand)[P_TOPK - 1]
        top = v1[0] + v2[0]
        z = jnp.zeros_like(tau)
        cnt_a = [jnp.zeros_like(tau) for _ in range(P_TOPK)]
        for a, b in _CAND:
            sm = v1[a] + v2[b]
            sel = sm >= tau
            z = z + jnp.where(sel, jnp.exp(sm - top), 0.0)
            cnt_a[a] = cnt_a[a] + jnp.where(sel, 1.0, 0.0)
        cnt_b = [jnp.broadcast_to(t, (_SUBLANES, t.shape[1])) for t in cnt_a]
        for i in range(ng):
            rows = slice(i * _SUBLANES, (i + 1) * _SUBLANES)
            x1, x2 = s1[rows, :], s2[rows, :]
            r2 = jnp.full_like(x2, float(P_TOPK))
            cnt = jnp.zeros_like(x1)
            for a in reversed(range(P_TOPK)):
                r2 = jnp.where(x2 >= v2s[a], float(a), r2)
                cnt = jnp.where(x1 == v1s[a], cnt_b[a], cnt)
            r2_ref[0, rows, lanes] = r2
            cnt_ref[0, rows, lanes] = cnt
        e2_ref[0, :, lanes] = jnp.exp(s2 - v2[0])
        f_ref[0, :, lanes] = jnp.exp(s1 - v1[0]) / z


def _peer_route(s1t, s2t, tl):
    T = s1t.shape[2]
    spec = pl.BlockSpec((1, P_NK, tl), lambda h, i: (h, 0, i))
    shp32 = jax.ShapeDtypeStruct((P_H, P_NK, T), F32)
    return pl.pallas_call(
        _route_kernel,
        grid=(P_H, T // tl),
        in_specs=[spec, spec],
        out_specs=[spec, spec, spec, spec],
        out_shape=[shp32, shp32, shp32, shp32],
        compiler_params=_cparams(("parallel", "parallel")),
        name="peer_route",
    )(s1t, s2t)


def _gelu2(x):
    return x * (1.0 + lax.erf(x * (2.0 ** -0.5)))


_LANES = 128
_PACK = 16
_N1_GROUP = 1


def _peer_kernel(h2_ref, x1_ref, u_ref, v_ref, r2f_ref, e2f_ref, cnt_ref, f_ref, o_ref,
                 acc_ref, a_ref, p_ref, r2_ref, e2_ref, *, nb, tm):
    e = pl.program_id(1)

    @pl.when(e == 0)
    def _():
        acc_ref[...] = jnp.zeros_like(acc_ref)
        for h in range(P_H):
            r2_ref[h] = r2f_ref[h].astype(BF16)
            e2_ref[h] = e2f_ref[h].astype(BF16)

    a_ref[...] = _dot_nt(u_ref[...], h2_ref[...])
    n1_0 = pl.multiple_of(e * nb, nb)
    zero = jnp.zeros((_PACK, _LANES), BF16)
    for g in range(tm // _LANES):
        lanes = slice(g * _LANES, (g + 1) * _LANES)
        for jc in range(nb // _N1_GROUP):
            w = [[None] * (P_NK // _PACK) for _ in range(_N1_GROUP)]
            for h in range(P_H):
                cnt8 = cnt_ref[h, pl.ds(n1_0, nb), lanes]
                f8 = 0.5 * f_ref[h, pl.ds(n1_0, nb), lanes]
                r2t = [r2_ref[h, r * _PACK:(r + 1) * _PACK, lanes] for r in range(P_NK // _PACK)]
                e2t = [e2_ref[h, r * _PACK:(r + 1) * _PACK, lanes] for r in range(P_NK // _PACK)]
                for jj in range(_N1_GROUP):
                    j = jc * _N1_GROUP + jj
                    cb = jnp.broadcast_to(cnt8[j:j + 1, :], (_PACK, _LANES)).astype(BF16)
                    fb = jnp.broadcast_to(f8[j:j + 1, :], (_PACK, _LANES)).astype(BF16)
                    for r in range(P_NK // _PACK):
                        t = jnp.where(r2t[r] < cb, e2t[r], zero) * fb
                        w[jj][r] = t if w[jj][r] is None else w[jj][r] + t
            for jj in range(_N1_GROUP):
                j = jc * _N1_GROUP + jj
                for r in range(P_NK // _PACK):
                    rows = slice(j * P_NK + r * _PACK, j * P_NK + (r + 1) * _PACK)
                    p_ref[rows, lanes] = w[jj][r] * _gelu2(a_ref[rows, lanes]).astype(BF16)
    acc_ref[...] += _dot_tn(p_ref[...], v_ref[...])

    @pl.when(e == pl.num_programs(1) - 1)
    def _():
        o_ref[...] = x1_ref[...] + acc_ref[...]


def _peer_mlp(h2, x1, u, v, r2t, e2t, cntt, ft, tm, nb):
    T = x1.shape[0]
    ne = P_NK // nb
    te = nb * P_NK
    rspec = pl.BlockSpec((P_H, P_NK, tm), lambda i, e: (0, 0, i))
    return pl.pallas_call(
        functools.partial(_peer_kernel, nb=nb, tm=tm),
        grid=(T // tm, ne),
        in_specs=[pl.BlockSpec((tm, D_MODEL), lambda i, e: (i, 0)),
                  pl.BlockSpec((tm, D_MODEL), lambda i, e: (i, 0)),
                  pl.BlockSpec((te, D_MODEL), lambda i, e: (e, 0)),
                  pl.BlockSpec((te, D_MODEL), lambda i, e: (e, 0)),
                  rspec, rspec, rspec, rspec],
        out_specs=pl.BlockSpec((tm, D_MODEL), lambda i, e: (i, 0)),
        out_shape=jax.ShapeDtypeStruct((T, D_MODEL), F32),
        scratch_shapes=[pltpu.VMEM((tm, D_MODEL), F32),
                        pltpu.VMEM((te, tm), F32),
                        pltpu.VMEM((te, tm), BF16),
                        pltpu.VMEM((P_H, P_NK, tm), BF16),
                        pltpu.VMEM((P_H, P_NK, tm), BF16)],
        compiler_params=_cparams(("parallel", "arbitrary")),
        name="peer_mlp",
    )(h2, x1, u, v, r2t, e2t, cntt, ft)


def _swap_halves(t, axis=-1):
    a, b = jnp.split(t, 2, axis=axis)
    return jnp.concatenate([b, a], axis=axis)


def _pad_cols(t, n):
    return jnp.pad(t, ((0, 0), (0, n - t.shape[1])))


def _layer(x2, pos_col, B, S, attn_norm_g, w_in, b_igate, b_fgate, mlstm_norm_g, cq_norm_g, w_uq,
           ckv_norm_g, w_ukv, q_norm_g, k_norm_g, w_out, ffn_norm_g, peer_w_q, peer_keys_1,
           peer_keys_2, peer_u, peer_v):
    T = B * S
    n_qkvo = 4 * M_H * M_DK
    o_gate = n_qkvo
    o_cq = o_gate + 2 * M_H
    o_ckv = o_cq + A_QR
    o_kr = o_ckv + A_KVR

    w_main = w_in[:, :n_qkvo].astype(BF16)
    w_kr = w_in[:, o_kr:o_kr + A_ROPE]
    w_lat = jnp.concatenate([w_in[:, o_cq:o_kr], _pad_cols(w_kr, 128),
                             _pad_cols(_swap_halves(w_kr), 128)], axis=1).astype(BF16)
    w_gate = _pad_cols(w_in[:, o_gate:o_gate + 2 * M_H], 128).astype(BF16)
    bias_row = _pad_cols(jnp.concatenate([b_igate, b_fgate])[None, :], 128)

    wq_h = w_uq.reshape(A_QR, A_H, A_DQK)
    wq_pad = jnp.pad(wq_h, ((0, 0), (0, 0), (0, A_DPAD - A_DQK))).reshape(A_QR, A_H * A_DPAD).astype(BF16)
    wq_sw = jnp.pad(_swap_halves(wq_h[:, :, A_NOPE:]), ((0, 0), (0, 0), (0, 128 - A_ROPE)))
    wq_sw = wq_sw.reshape(A_QR, A_H * 128).astype(BF16)
    w_kv = w_ukv.astype(BF16)
    qg = _pad_cols(q_norm_g[None, :], A_DPAD)
    qgs = _pad_cols(_swap_halves(q_norm_g[A_NOPE:])[None, :], 128)
    kg = _pad_cols(k_norm_g[None, :], A_DPAD)
    kgs = _pad_cols(_swap_halves(k_norm_g[A_NOPE:])[None, :], 128)
    inv_freq = ROPE_THETA ** (-jnp.arange(0, A_ROPE, 2, dtype=F32) / A_ROPE)
    freq_row = _pad_cols(jnp.concatenate([inv_freq, inv_freq])[None, :], 128)
    half = A_ROPE // 2
    sign_row = _pad_cols(jnp.concatenate([-jnp.ones((half,), F32), jnp.ones((half,), F32)])[None, :], 128)

    qkvo, lat, gates = _in_proj(x2, attn_norm_g[None, :], w_main, w_lat, w_gate, tm=min(512, T))
    y_m = _mlstm(qkvo, gates, bias_row, mlstm_norm_g[None, :], B, S, L=min(256, S), nbatch=1)
    consts = [cq_norm_g[None, :], ckv_norm_g[None, :], wq_pad, wq_sw, w_kv, qg, qgs, kg, kgs,
              freq_row, sign_row]
    q, k, v = _mla_prep(lat, pos_col, consts, tm=min(512, T))
    y_a = _mla_attn(q, k, v, B, S, tq=min(1024, S), hp=2)
    wo = w_out.astype(BF16)
    x1, h2, s1t, s2t = _out_proj(x2, y_m, y_a, wo[:M_H * M_DV], wo[M_H * M_DV:], ffn_norm_g[None, :],
                                  peer_w_q.astype(BF16), peer_keys_1.astype(BF16),
                                  peer_keys_2.astype(BF16), tm=min(512, T))
    r2t, e2t, cntt, ft = _peer_route(s1t, s2t, tl=min(512, T))
    return _peer_mlp(h2, x1, peer_u.astype(BF16), peer_v.astype(BF16), r2t, e2t, cntt, ft,
                     tm=min(512, T), nb=8)


def kernel(x, positions, attn_norm_g, w_in, b_igate, b_fgate, mlstm_norm_g, cq_norm_g, w_uq, ckv_norm_g,
           w_ukv, q_norm_g, k_norm_g, w_out, ffn_norm_g, peer_w_q, peer_keys_1, peer_keys_2, peer_u, peer_v):
    B, S, D = x.shape
    x2 = x.reshape(B * S, D)
    pos_col = positions.reshape(B * S, 1).astype(F32)
    for l in range(attn_norm_g.shape[0]):
        x2 = _layer(x2, pos_col, B, S, attn_norm_g[l], w_in[l], b_igate[l], b_fgate[l], mlstm_norm_g[l],
                    cq_norm_g[l], w_uq[l], ckv_norm_g[l], w_ukv[l], q_norm_g[l], k_norm_g[l], w_out[l],
                    ffn_norm_g[l], peer_w_q[l], peer_keys_1[l], peer_keys_2[l], peer_u[l], peer_v[l])
    return x2.reshape(B, S, D)
```

```python
import functools

import jax
import jax.numpy as jnp
from jax import lax
from jax.experimental import pallas as pl
from jax.experimental.pallas import tpu as pltpu

F32 = jnp.float32
BF16 = jnp.bfloat16

D_MODEL = 1024
RMS_EPS = 1e-6
M_H, M_DK, M_DV = 4, 128, 128
A_H, A_QR, A_KVR, A_NOPE, A_ROPE, A_DV = 4, 256, 128, 128, 64, 128
A_DQK = A_NOPE + A_ROPE
A_DPAD = 256
ROPE_THETA = 10000.0
P_H, P_NK, P_DK, P_TOPK = 8, 128, 256, 16
NEG_INF = float("-inf")

VMEM_LIMIT = 56 * 1024 * 1024


def _cparams(sem):
    return pltpu.CompilerParams(dimension_semantics=sem, vmem_limit_bytes=VMEM_LIMIT)


def _dot(a, b):
    return jnp.dot(a, b, preferred_element_type=F32)


def _dot_nt(a, b):
    return lax.dot_general(a, b, (((1,), (1,)), ((), ())), preferred_element_type=F32)


def _dot_tn(a, b):
    return lax.dot_general(a, b, (((0,), (0,)), ((), ())), preferred_element_type=F32)


def _inproj_kernel(x_ref, g_ref, wm_ref, wl_ref, wg_ref, qkvo_ref, lat_ref, gate_ref):
    x = x_ref[...]
    ms = jnp.mean(x * x, axis=-1, keepdims=True)
    h = (x * lax.rsqrt(ms + RMS_EPS) * g_ref[...]).astype(BF16)
    main = _dot(h, wm_ref[...])
    kcol = lax.broadcasted_iota(jnp.int32, (1, main.shape[1]), 1) // (M_H * M_DK) == 1
    main = jnp.where(kcol, main * (M_DK ** -0.5), main)
    qkvo_ref[...] = main.astype(BF16)
    lat_ref[...] = _dot(h, wl_ref[...])
    gate_ref[...] = _dot(h, wg_ref[...])


def _in_proj(x2, g, w_main, w_lat, w_gate, tm):
    T = x2.shape[0]
    n_main, n_lat, n_gate = w_main.shape[1], w_lat.shape[1], w_gate.shape[1]
    full = lambda shape: pl.BlockSpec(shape, lambda i: (0, 0))
    return pl.pallas_call(
        _inproj_kernel,
        grid=(T // tm,),
        in_specs=[pl.BlockSpec((tm, D_MODEL), lambda i: (i, 0)), full((1, D_MODEL)),
                  full(w_main.shape), full(w_lat.shape), full(w_gate.shape)],
        out_specs=[pl.BlockSpec((tm, n_main), lambda i: (i, 0)),
                   pl.BlockSpec((tm, n_lat), lambda i: (i, 0)),
                   pl.BlockSpec((tm, n_gate), lambda i: (i, 0))],
        out_shape=[jax.ShapeDtypeStruct((T, n_main), BF16),
                   jax.ShapeDtypeStruct((T, n_lat), F32),
                   jax.ShapeDtypeStruct((T, n_gate), F32)],
        compiler_params=_cparams(("parallel",)),
        name="in_proj",
    )(x2, g, w_main, w_lat, w_gate)


def _log_sigmoid(x):
    return -(jnp.maximum(-x, 0.0) + jnp.log1p(jnp.exp(-jnp.abs(x))))


def _mlstm_kernel(qkvo_ref, gate_ref, bias_ref, ng_ref, y_ref, c_ref, m_ref, *, L, nbatch):
    c_idx = pl.program_id(1)

    @pl.when(c_idx == 0)
    def _():
        c_ref[...] = jnp.zeros_like(c_ref)
        m_ref[...] = jnp.zeros_like(m_ref)

    H, DK, DV = M_H, M_DK, M_DV
    row = lax.broadcasted_iota(jnp.int32, (L, L), 0)
    col = lax.broadcasted_iota(jnp.int32, (L, L), 1)
    causal = col <= row
    tri = jnp.where(causal, 1.0, 0.0)
    lane = lax.broadcasted_iota(jnp.int32, (L, 128), 1)
    ones_col = jnp.where(lane == 0, 1.0, 0.0).astype(BF16)

    for bb in range(nbatch):
        G = gate_ref[bb] + bias_ref[...]
        LF = _log_sigmoid(G)
        Bc = jnp.dot(tri, LF, preferred_element_type=F32, precision=lax.Precision.HIGHEST)
        ZT = jnp.where(lane < H, G, Bc).T

        for h in range(H):
            q = qkvo_ref[bb, :, h * DK:(h + 1) * DK]
            k = qkvo_ref[bb, :, H * DK + h * DK: H * DK + (h + 1) * DK]
            v = qkvo_ref[bb, :, 2 * H * DK + h * DV: 2 * H * DK + (h + 1) * DV]
            o = qkvo_ref[bb, :, 3 * H * DK + h * DV: 3 * H * DK + (h + 1) * DV]
            i_col = G[:, h:h + 1]
            b_col = Bc[:, H + h:H + h + 1]
            i_row = ZT[h:h + 1, :]
            b_row = ZT[H + h:H + h + 1, :]
            m_prev = m_ref[bb * H + h]
            c_prev = c_ref[bb * H + h]

            a_col = b_col + m_prev
            D = jnp.where(causal, b_col - b_row + i_row, NEG_INF)
            m_t = jnp.maximum(a_col, jnp.max(D, axis=-1, keepdims=True))
            Dw = jnp.exp(D - m_t)
            aw = jnp.exp(a_col - m_t)
            s = (_dot_nt(q, k) * Dw).astype(BF16)
            v_aug = jnp.concatenate([v, ones_col], axis=-1)
            num = aw * _dot(q, c_prev.astype(BF16)) + _dot(s, v_aug)
            den = num[:, DV:DV + 1]
            hval = num[:, :DV] / jnp.maximum(jnp.abs(den), jnp.exp(-m_t))

            bL = b_col[L - 1:L, :]
            g_col = bL - b_col + i_col
            m_new = jnp.maximum(bL + m_prev, jnp.max(g_col, axis=0, keepdims=True))
            decay = jnp.exp(bL + m_prev - m_new)
            w_col = jnp.exp(g_col - m_new)
            wv = (w_col * v_aug.astype(F32)).astype(BF16)
            c_ref[bb * H + h] = decay * c_prev + _dot_tn(k, wv)
            m_ref[bb * H + h] = m_new

            ms = jnp.mean(hval * hval, axis=-1, keepdims=True)
            hn = hval * lax.rsqrt(ms + RMS_EPS) * ng_ref[:, h * DV:(h + 1) * DV]
            y_ref[bb, :, h * DV:(h + 1) * DV] = (jax.nn.sigmoid(o.astype(F32)) * hn).astype(BF16)


def _mlstm(qkvo, gates, bias_row, norm_g, B, S, L, nbatch):
    nc = S // L
    n_in = qkvo.shape[1]
    y = pl.pallas_call(
        functools.partial(_mlstm_kernel, L=L, nbatch=nbatch),
        grid=(B // nbatch, nc),
        in_specs=[pl.BlockSpec((nbatch, L, n_in), lambda b, c: (b, c, 0)),
                  pl.BlockSpec((nbatch, L, 128), lambda b, c: (b, c, 0)),
                  pl.BlockSpec((1, 128), lambda b, c: (0, 0)),
                  pl.BlockSpec((1, M_H * M_DV), lambda b, c: (0, 0))],
        out_specs=pl.BlockSpec((nbatch, L, M_H * M_DV), lambda b, c: (b, c, 0)),
        out_shape=jax.ShapeDtypeStruct((B, S, M_H * M_DV), BF16),
        scratch_shapes=[pltpu.VMEM((nbatch * M_H, M_DK, 2 * M_DV), F32),
                        pltpu.VMEM((nbatch * M_H, 1, 1), F32)],
        compiler_params=_cparams(("parallel", "arbitrary")),
        name="mlstm",
    )(qkvo.reshape(B, S, n_in), gates.reshape(B, S, 128), bias_row, norm_g)
    return y.reshape(B * S, M_H * M_DV)


def _mla_prep_kernel(lat_ref, pos_ref, cqg_ref, ckvg_ref, wq_ref, wqs_ref, wkv_ref,
                     qg_ref, qgs_ref, kg_ref, kgs_ref, freq_ref, sign_ref,
                     q_ref, k_ref, v_ref):
    H = A_H
    cq = lat_ref[:, :A_QR]
    ckv = lat_ref[:, A_QR:A_QR + A_KVR]
    kr = lat_ref[:, A_QR + A_KVR:A_QR + A_KVR + 128]
    krs = lat_ref[:, A_QR + A_KVR + 128:A_QR + A_KVR + 256]

    def rms(t, g):
        return t * lax.rsqrt(jnp.mean(t * t, axis=-1, keepdims=True) + RMS_EPS) * g

    cqn = rms(cq, cqg_ref[...]).astype(BF16)
    ckvn = rms(ckv, ckvg_ref[...]).astype(BF16)
    qf = _dot(cqn, wq_ref[...])
    qs = _dot(cqn, wqs_ref[...])
    kv = _dot(ckvn, wkv_ref[...])

    ang = pos_ref[...] * freq_ref[...]
    cos = jnp.cos(ang)
    sin = jnp.sin(ang) * sign_ref[...]
    scale = A_DQK ** -0.5
    kr_ss = jnp.sum(kr * kr, axis=-1, keepdims=True)

    for h in range(H):
        qh = qf[:, h * A_DPAD:(h + 1) * A_DPAD]
        rstd = lax.rsqrt(jnp.sum(qh * qh, axis=-1, keepdims=True) / A_DQK + RMS_EPS) * scale
        q_nope = qh[:, :128] * rstd * qg_ref[:, :128]
        q_rope = qh[:, 128:] * rstd * qg_ref[:, 128:]
        q_rope_s = qs[:, h * 128:(h + 1) * 128] * rstd * qgs_ref[...]
        q_ref[:, h * A_DPAD:h * A_DPAD + 128] = q_nope.astype(BF16)
        q_ref[:, h * A_DPAD + 128:(h + 1) * A_DPAD] = (q_rope * cos + q_rope_s * sin).astype(BF16)

        kn = kv[:, h * 256:h * 256 + 128]
        rstd_k = lax.rsqrt((jnp.sum(kn * kn, axis=-1, keepdims=True) + kr_ss) / A_DQK + RMS_EPS)
        k_nope = kn * rstd_k * kg_ref[:, :128]
        k_rope = kr * rstd_k * kg_ref[:, 128:]
        k_rope_s = krs * rstd_k * kgs_ref[...]
        k_ref[:, h * A_DPAD:h * A_DPAD + 128] = k_nope.astype(BF16)
        k_ref[:, h * A_DPAD + 128:(h + 1) * A_DPAD] = (k_rope * cos + k_rope_s * sin).astype(BF16)
        v_ref[:, h * A_DV:(h + 1) * A_DV] = kv[:, h * 256 + 128:(h + 1) * 256].astype(BF16)


def _mla_prep(lat, pos_col, consts, tm):
    T = lat.shape[0]
    full = lambda a: pl.BlockSpec(a.shape, lambda i: (0, 0))
    return pl.pallas_call(
        _mla_prep_kernel,
        grid=(T // tm,),
        in_specs=[pl.BlockSpec((tm, lat.shape[1]), lambda i: (i, 0)),
                  pl.BlockSpec((tm, 1), lambda i: (i, 0))] + [full(c) for c in consts],
        out_specs=[pl.BlockSpec((tm, A_H * A_DPAD), lambda i: (i, 0)),
                   pl.BlockSpec((tm, A_H * A_DPAD), lambda i: (i, 0)),
                   pl.BlockSpec((tm, A_H * A_DV), lambda i: (i, 0))],
        out_shape=[jax.ShapeDtypeStruct((T, A_H * A_DPAD), BF16),
                   jax.ShapeDtypeStruct((T, A_H * A_DPAD), BF16),
                   jax.ShapeDtypeStruct((T, A_H * A_DV), BF16)],
        compiler_params=_cparams(("parallel",)),
        name="mla_prep",
    )(lat, pos_col, *consts)


def _attn_kernel(q_ref, k_ref, v_ref, o_ref, *, tq, hp):
    qi = pl.program_id(2)

    def chunk(j, carry, masked):
        rows = pl.ds(pl.multiple_of(j * tq, tq), tq)
        out = []
        for h in range(hp):
            m, l, acc = carry[h]
            q = q_ref[:, h * A_DPAD:(h + 1) * A_DPAD]
            k = k_ref[rows, h * A_DPAD:(h + 1) * A_DPAD]
            v = v_ref[rows, h * A_DV:(h + 1) * A_DV]
            s = _dot_nt(q, k)
            if masked:
                row = lax.broadcasted_iota(jnp.int32, (tq, tq), 0)
                col = lax.broadcasted_iota(jnp.int32, (tq, tq), 1)
                s = jnp.where(col <= row, s, NEG_INF)
            m_new = jnp.maximum(m, jnp.max(s, axis=-1, keepdims=True))
            alpha = jnp.exp(m - m_new)
            p = jnp.exp(s - m_new)
            l = alpha * l + jnp.sum(p, axis=-1, keepdims=True)
            acc = alpha * acc + _dot(p.astype(BF16), v)
            out.append((m_new, l, acc))
        return tuple(out)

    init = tuple((jnp.full((tq, 1), NEG_INF, F32), jnp.zeros((tq, 1), F32), jnp.zeros((tq, A_DV), F32))
                 for _ in range(hp))
    carry = lax.fori_loop(0, qi, lambda j, c: chunk(j, c, False), init)
    carry = chunk(qi, carry, True)
    for h in range(hp):
        m, l, acc = carry[h]
        o_ref[:, h * A_DV:(h + 1) * A_DV] = (acc / l).astype(BF16)


def _mla_attn(q, k, v, B, S, tq, hp):
    T = B * S
    nq = S // tq
    return pl.pallas_call(
        functools.partial(_attn_kernel, tq=tq, hp=hp),
        grid=(B, A_H // hp, nq),
        in_specs=[pl.BlockSpec((tq, hp * A_DPAD), lambda b, h, i: (b * nq + i, h)),
                  pl.BlockSpec((S, hp * A_DPAD), lambda b, h, i: (b, h)),
                  pl.BlockSpec((S, hp * A_DV), lambda b, h, i: (b, h))],
        out_specs=pl.BlockSpec((tq, hp * A_DV), lambda b, h, i: (b * nq + i, h)),
        out_shape=jax.ShapeDtypeStruct((T, A_H * A_DV), BF16),
        compiler_params=_cparams(("parallel", "parallel", "arbitrary")),
        name="mla_attn",
    )(q, k, v)


def _outproj_kernel(x_ref, ym_ref, ya_ref, wom_ref, woa_ref, g_ref, wq_ref, k1_ref, k2_ref,
                    x1_ref, h2_ref, s1_ref, s2_ref):
    x1 = x_ref[...] + _dot(ym_ref[...], wom_ref[...]) + _dot(ya_ref[...], woa_ref[...])
    x1_ref[...] = x1
    ms = jnp.mean(x1 * x1, axis=-1, keepdims=True)
    h2f = x1 * lax.rsqrt(ms + RMS_EPS) * g_ref[...]
    h2 = h2f.astype(BF16)
    h2_ref[...] = h2
    qry = _dot(h2, wq_ref[...]).astype(BF16)
    half = P_DK // 2
    for h in range(P_H):
        s1_ref[h] = _dot_nt(k1_ref[...], qry[:, h * P_DK:h * P_DK + half])
        s2_ref[h] = _dot_nt(k2_ref[...], qry[:, h * P_DK + half:(h + 1) * P_DK])


def _out_proj(x2, ym, ya, wo_m, wo_a, g, wq, k1, k2, tm):
    T = x2.shape[0]
    full = lambda a: pl.BlockSpec(a.shape, lambda i: (0, 0))
    return pl.pallas_call(
        _outproj_kernel,
        grid=(T // tm,),
        in_specs=[pl.BlockSpec((tm, D_MODEL), lambda i: (i, 0)),
                  pl.BlockSpec((tm, ym.shape[1]), lambda i: (i, 0)),
                  pl.BlockSpec((tm, ya.shape[1]), lambda i: (i, 0)),
                  full(wo_m), full(wo_a), full(g), full(wq), full(k1), full(k2)],
        out_specs=[pl.BlockSpec((tm, D_MODEL), lambda i: (i, 0)),
                   pl.BlockSpec((tm, D_MODEL), lambda i: (i, 0)),
                   pl.BlockSpec((P_H, P_NK, tm), lambda i: (0, 0, i)),
                   pl.BlockSpec((P_H, P_NK, tm), lambda i: (0, 0, i))],
        out_shape=[jax.ShapeDtypeStruct((T, D_MODEL), F32),
                   jax.ShapeDtypeStruct((T, D_MODEL), BF16),
                   jax.ShapeDtypeStruct((P_H, P_NK, T), F32),
                   jax.ShapeDtypeStruct((P_H, P_NK, T), F32)],
        compiler_params=_cparams(("parallel",)),
        name="out_peerq",
    )(x2, ym, ya, wo_m, wo_a, g, wq, k1, k2)


def _top16(x):
    vals = []
    for _ in range(P_TOPK):
        m = jnp.max(x, axis=0, keepdims=True)
        vals.append(m)
        x = jnp.where(x == m, NEG_INF, x)
    return vals


def _oddeven_merge(lo, hi, r):
    step = r * 2
    if step < hi - lo:
        yield from _oddeven_merge(lo, hi, step)
        yield from _oddeven_merge(lo + r, hi, step)
        yield from [(i, i + r) for i in range(lo + r, hi - r, step)]
    else:
        yield (lo, lo + r)


def _oddeven_merge_sort(lo, hi):
    if hi - lo >= 1:
        mid = lo + (hi - lo) // 2
        yield from _oddeven_merge_sort(lo, mid)
        yield from _oddeven_merge_sort(mid + 1, hi)
        yield from _oddeven_merge(lo, hi, 1)


_SORT16 = tuple(_oddeven_merge_sort(0, P_TOPK - 1))
_SUBLANES = 8


def _cmpx(v, i, j):
    a, b = v[i], v[j]
    v[i] = jnp.maximum(a, b)
    v[j] = jnp.minimum(a, b)


def _top16_sorted(x):
    v = [x[i * _SUBLANES:(i + 1) * _SUBLANES, :] for i in range(P_TOPK)]
    for i, j in _SORT16:
        _cmpx(v, i, j)
    for shift in (4, 2, 1):
        o = [pltpu.roll(t, shift, axis=0) for t in v]
        v = [jnp.maximum(v[i], o[P_TOPK - 1 - i]) for i in range(P_TOPK)]
        for d in (8, 4, 2, 1):
            for i in range(P_TOPK):
                if not i & d:
                    _cmpx(v, i, i + d)
    return v


_CAND = [(a, b) for a in range(P_TOPK) for b in range(P_TOPK) if (a + 1) * (b + 1) <= P_TOPK]


_ROUTE_LANES = 256


def _route_kernel(s1_ref, s2_ref, r2_ref, e2_ref, cnt_ref, f_ref):
    ng = P_NK // _SUBLANES
    for c in range(s1_ref.shape[2] // _ROUTE_LANES):
        lanes = slice(c * _ROUTE_LANES, (c + 1) * _ROUTE_LANES)
        s1 = s1_ref[0, :, lanes]
        s2 = s2_ref[0, :, lanes]
        v1s = _top16_sorted(s1)
        v2s = _top16_sorted(s2)
        v1 = [t[0:1, :] for t in v1s]
        v2 = [t[0:1, :] for t in v2s]
        cand = jnp.concatenate([v1[a] + v2[b] for a, b in _CAND], axis=0)
        tau = _top16(cand)[P_TOPK - 1]
        top = v1[0] + v2[0]
        z = jnp.zeros_like(tau)
        cnt_a = [jnp.zeros_like(tau) for _ in range(P_TOPK)]
        for a, b in _CAND:
            sm = v1[a] + v2[b]
            sel = sm >= tau
            z = z + jnp.where(sel, jnp.exp(sm - top), 0.0)
            cnt_a[a] = cnt_a[a] + jnp.where(sel, 1.0, 0.0)
        cnt_b = [jnp.broadcast_to(t, (_SUBLANES, t.shape[1])) for t in cnt_a]
        for i in range(ng):
            rows = slice(i * _SUBLANES, (i + 1) * _SUBLANES)
            x1, x2 = s1[rows, :], s2[rows, :]
            r2 = jnp.full_like(x2, float(P_TOPK))
            cnt = jnp.zeros_like(x1)
            for a in reversed(range(P_TOPK)):
                r2 = jnp.where(x2 >= v2s[a], float(a), r2)
                cnt = jnp.where(x1 == v1s[a], cnt_b[a], cnt)
            r2_ref[0, rows, lanes] = r2
            cnt_ref[0, rows, lanes] = cnt
        e2_ref[0, :, lanes] = jnp.exp(s2 - v2[0])
        f_ref[0, :, lanes] = jnp.exp(s1 - v1[0]) / z


def _peer_route(s1t, s2t, tl):
    T = s1t.shape[2]
    spec = pl.BlockSpec((1, P_NK, tl), lambda h, i: (h, 0, i))
    shp32 = jax.ShapeDtypeStruct((P_H, P_NK, T), F32)
    return pl.pallas_call(
        _route_kernel,
        grid=(P_H, T // tl),
        in_specs=[spec, spec],
        out_specs=[spec, spec, spec, spec],
        out_shape=[shp32, shp32, shp32, shp32],
        compiler_params=_cparams(("parallel", "parallel")),
        name="peer_route",
    )(s1t, s2t)


def _gelu2(x):
    return x * (1.0 + lax.erf(x * (2.0 ** -0.5)))


_LANES = 128
_PACK = 16
_N1_GROUP = 1


def _peer_kernel(h2_ref, x1_ref, u_ref, v_ref, r2f_ref, e2f_ref, cnt_ref, f_ref, o_ref,
                 acc_ref, a_ref, p_ref, r2_ref, e2_ref, *, nb, tm):
    e = pl.program_id(1)

    @pl.when(e == 0)
    def _():
        acc_ref[...] = jnp.zeros_like(acc_ref)
        for h in range(P_H):
            r2_ref[h] = r2f_ref[h].astype(BF16)
            e2_ref[h] = e2f_ref[h].astype(BF16)

    a_ref[...] = _dot_nt(u_ref[...], h2_ref[...])
    n1_0 = pl.multiple_of(e * nb, nb)
    zero = jnp.zeros((_PACK, _LANES), BF16)
    for g in range(tm // _LANES):
        lanes = slice(g * _LANES, (g + 1) * _LANES)
        for jc in range(nb // _N1_GROUP):
            w = [[None] * (P_NK // _PACK) for _ in range(_N1_GROUP)]
            for h in range(P_H):
                cnt8 = cnt_ref[h, pl.ds(n1_0, nb), lanes]
                f8 = 0.5 * f_ref[h, pl.ds(n1_0, nb), lanes]
                r2t = [r2_ref[h, r * _PACK:(r + 1) * _PACK, lanes] for r in range(P_NK // _PACK)]
                e2t = [e2_ref[h, r * _PACK:(r + 1) * _PACK, lanes] for r in range(P_NK // _PACK)]
                for jj in range(_N1_GROUP):
                    j = jc * _N1_GROUP + jj
                    cb = jnp.broadcast_to(cnt8[j:j + 1, :], (_PACK, _LANES)).astype(BF16)
                    fb = jnp.broadcast_to(f8[j:j + 1, :], (_PACK, _LANES)).astype(BF16)
                    for r in range(P_NK // _PACK):
                        t = jnp.where(r2t[r] < cb, e2t[r], zero) * fb
                        w[jj][r] = t if w[jj][r] is None else w[jj][r] + t
            for jj in range(_N1_GROUP):
                j = jc * _N1_GROUP + jj
                for r in range(P_NK // _PACK):
                    rows = slice(j * P_NK + r * _PACK, j * P_NK + (r + 1) * _PACK)
                    p_ref[rows, lanes] = w[jj][r] * _gelu2(a_ref[rows, lanes]).astype(BF16)
    acc_ref[...] += _dot_tn(p_ref[...], v_ref[...])

    @pl.when(e == pl.num_programs(1) - 1)
    def _():
        o_ref[...] = x1_ref[...] + acc_ref[...]


def _peer_mlp(h2, x1, u, v, r2t, e2t, cntt, ft, tm, nb):
    T = x1.shape[0]
    ne = P_NK // nb
    te = nb * P_NK
    rspec = pl.BlockSpec((P_H, P_NK, tm), lambda i, e: (0, 0, i))
    return pl.pallas_call(
        functools.partial(_peer_kernel, nb=nb, tm=tm),
        grid=(T // tm, ne),
        in_specs=[pl.BlockSpec((tm, D_MODEL), lambda i, e: (i, 0)),
                  pl.BlockSpec((tm, D_MODEL), lambda i, e: (i, 0)),
                  pl.BlockSpec((te, D_MODEL), lambda i, e: (e, 0)),
                  pl.BlockSpec((te, D_MODEL), lambda i, e: (e, 0)),
                  rspec, rspec, rspec, rspec],
        out_specs=pl.BlockSpec((tm, D_MODEL), lambda i, e: (i, 0)),
        out_shape=jax.ShapeDtypeStruct((T, D_MODEL), F32),
        scratch_shapes=[pltpu.VMEM((tm, D_MODEL), F32),
                        pltpu.VMEM((te, tm), F32),
                        pltpu.VMEM((te, tm), BF16),
                        pltpu.VMEM((P_H, P_NK, tm), BF16),
                        pltpu.VMEM((P_H, P_NK, tm), BF16)],
        compiler_params=_cparams(("parallel", "arbitrary")),
        name="peer_mlp",
    )(h2, x1, u, v, r2t, e2t, cntt, ft)


def _swap_halves(t, axis=-1):
    a, b = jnp.split(t, 2, axis=axis)
    return jnp.concatenate([b, a], axis=axis)


def _pad_cols(t, n):
    return jnp.pad(t, ((0, 0), (0, n - t.shape[1])))


def _layer(x2, pos_col, B, S, attn_norm_g, w_in, b_igate, b_fgate, mlstm_norm_g, cq_norm_g, w_uq,
           ckv_norm_g, w_ukv, q_norm_g, k_norm_g, w_out, ffn_norm_g, peer_w_q, peer_keys_1,
           peer_keys_2, peer_u, peer_v):
    T = B * S
    n_qkvo = 4 * M_H * M_DK
    o_gate = n_qkvo
    o_cq = o_gate + 2 * M_H
    o_ckv = o_cq + A_QR
    o_kr = o_ckv + A_KVR

    w_main = w_in[:, :n_qkvo].astype(BF16)
    w_kr = w_in[:, o_kr:o_kr + A_ROPE]
    w_lat = jnp.concatenate([w_in[:, o_cq:o_kr], _pad_cols(w_kr, 128),
                             _pad_cols(_swap_halves(w_kr), 128)], axis=1).astype(BF16)
    w_gate = _pad_cols(w_in[:, o_gate:o_gate + 2 * M_H], 128).astype(BF16)
    bias_row = _pad_cols(jnp.concatenate([b_igate, b_fgate])[None, :], 128)

    wq_h = w_uq.reshape(A_QR, A_H, A_DQK)
    wq_pad = jnp.pad(wq_h, ((0, 0), (0, 0), (0, A_DPAD - A_DQK))).reshape(A_QR, A_H * A_DPAD).astype(BF16)
    wq_sw = jnp.pad(_swap_halves(wq_h[:, :, A_NOPE:]), ((0, 0), (0, 0), (0, 128 - A_ROPE)))
    wq_sw = wq_sw.reshape(A_QR, A_H * 128).astype(BF16)
    w_kv = w_ukv.astype(BF16)
    qg = _pad_cols(q_norm_g[None, :], A_DPAD)
    qgs = _pad_cols(_swap_halves(q_norm_g[A_NOPE:])[None, :], 128)
    kg = _pad_cols(k_norm_g[None, :], A_DPAD)
    kgs = _pad_cols(_swap_halves(k_norm_g[A_NOPE:])[None, :], 128)
    inv_freq = ROPE_THETA ** (-jnp.arange(0, A_ROPE, 2, dtype=F32) / A_ROPE)
    freq_row = _pad_cols(jnp.concatenate([inv_freq, inv_freq])[None, :], 128)
    half = A_ROPE // 2
    sign_row = _pad_cols(jnp.concatenate([-jnp.ones((half,), F32), jnp.ones((half,), F32)])[None, :], 128)

    qkvo, lat, gates = _in_proj(x2, attn_norm_g[None, :], w_main, w_lat, w_gate, tm=min(512, T))
    y_m = _mlstm(qkvo, gates, bias_row, mlstm_norm_g[None, :], B, S, L=min(256, S), nbatch=1)
    consts = [cq_norm_g[None, :], ckv_norm_g[None, :], wq_pad, wq_sw, w_kv, qg, qgs, kg, kgs,
              freq_row, sign_row]
    q, k, v = _mla_prep(lat, pos_col, consts, tm=min(512, T))
    y_a = _mla_attn(q, k, v, B, S, tq=min(1024, S), hp=2)
    wo = w_out.astype(BF16)
    x1, h2, s1t, s2t = _out_proj(x2, y_m, y_a, wo[:M_H * M_DV], wo[M_H * M_DV:], ffn_norm_g[None, :],
                                  peer_w_q.astype(BF16), peer_keys_1.astype(BF16),
                                  peer_keys_2.astype(BF16), tm=min(512, T))
    r2t, e2t, cntt, ft = _peer_route(s1t, s2t, tl=min(1024, T))
    return _peer_mlp(h2, x1, peer_u.astype(BF16), peer_v.astype(BF16), r2t, e2t, cntt, ft,
                     tm=min(512, T), nb=8)


def kernel(x, positions, attn_norm_g, w_in, b_igate, b_fgate, mlstm_norm_g, cq_norm_g, w_uq, ckv_norm_g,
           w_ukv, q_norm_g, k_norm_g, w_out, ffn_norm_g, peer_w_q, peer_keys_1, peer_keys_2, peer_u, peer_v):
    B, S, D = x.shape
    x2 = x.reshape(B * S, D)
    pos_col = positions.reshape(B * S, 1).astype(F32)
    for l in range(attn_norm_g.shape[0]):
        x2 = _layer(x2, pos_col, B, S, attn_norm_g[l], w_in[l], b_igate[l], b_fgate[l], mlstm_norm_g[l],
                    cq_norm_g[l], w_uq[l], ckv_norm_g[l], w_ukv[l], q_norm_g[l], k_norm_g[l], w_out[l],
                    ffn_norm_g[l], peer_w_q[l], peer_keys_1[l], peer_keys_2[l], peer_u[l], peer_v[l])
    return x2.reshape(B, S, D)
```

```python
import functools

import jax
import jax.numpy as jnp
from jax import lax
from jax.experimental import pallas as pl
from jax.experimental.pallas import tpu as pltpu

F32 = jnp.float32
BF16 = jnp.bfloat16

D_MODEL = 1024
RMS_EPS = 1e-6
M_H, M_DK, M_DV = 4, 128, 128
A_H, A_QR, A_KVR, A_NOPE, A_ROPE, A_DV = 4, 256, 128, 128, 64, 128
A_DQK = A_NOPE + A_ROPE
A_DPAD = 256
ROPE_THETA = 10000.0
P_H, P_NK, P_DK, P_TOPK = 8, 128, 256, 16
NEG_INF = float("-inf")

VMEM_LIMIT = 56 * 1024 * 1024


def _cparams(sem):
    return pltpu.CompilerParams(dimension_semantics=sem, vmem_limit_bytes=VMEM_LIMIT)


def _dot(a, b):
    return jnp.dot(a, b, preferred_element_type=F32)


def _dot_nt(a, b):
    return lax.dot_general(a, b, (((1,), (1,)), ((), ())), preferred_element_type=F32)


def _dot_tn(a, b):
    return lax.dot_general(a, b, (((0,), (0,)), ((), ())), preferred_element_type=F32)


def _inproj_kernel(x_ref, g_ref, wm_ref, wl_ref, wg_ref, qkvo_ref, lat_ref, gate_ref):
    x = x_ref[...]
    ms = jnp.mean(x * x, axis=-1, keepdims=True)
    h = (x * lax.rsqrt(ms + RMS_EPS) * g_ref[...]).astype(BF16)
    main = _dot(h, wm_ref[...])
    kcol = lax.broadcasted_iota(jnp.int32, (1, main.shape[1]), 1) // (M_H * M_DK) == 1
    main = jnp.where(kcol, main * (M_DK ** -0.5), main)
    qkvo_ref[...] = main.astype(BF16)
    lat_ref[...] = _dot(h, wl_ref[...])
    gate_ref[...] = _dot(h, wg_ref[...])


def _in_proj(x2, g, w_main, w_lat, w_gate, tm):
    T = x2.shape[0]
    n_main, n_lat, n_gate = w_main.shape[1], w_lat.shape[1], w_gate.shape[1]
    full = lambda shape: pl.BlockSpec(shape, lambda i: (0, 0))
    return pl.pallas_call(
        _inproj_kernel,
        grid=(T // tm,),
        in_specs=[pl.BlockSpec((tm, D_MODEL), lambda i: (i, 0)), full((1, D_MODEL)),
                  full(w_main.shape), full(w_lat.shape), full(w_gate.shape)],
        out_specs=[pl.BlockSpec((tm, n_main), lambda i: (i, 0)),
                   pl.BlockSpec((tm, n_lat), lambda i: (i, 0)),
                   pl.BlockSpec((tm, n_gate), lambda i: (i, 0))],
        out_shape=[jax.ShapeDtypeStruct((T, n_main), BF16),
                   jax.ShapeDtypeStruct((T, n_lat), F32),
                   jax.ShapeDtypeStruct((T, n_gate), F32)],
        compiler_params=_cparams(("parallel",)),
        name="in_proj",
    )(x2, g, w_main, w_lat, w_gate)


def _log_sigmoid(x):
    return -(jnp.maximum(-x, 0.0) + jnp.log1p(jnp.exp(-jnp.abs(x))))


def _mlstm_kernel(qkvo_ref, gate_ref, bias_ref, ng_ref, y_ref, c_ref, m_ref, *, L, nbatch):
    c_idx = pl.program_id(1)

    @pl.when(c_idx == 0)
    def _():
        c_ref[...] = jnp.zeros_like(c_ref)
        m_ref[...] = jnp.zeros_like(m_ref)

    H, DK, DV = M_H, M_DK, M_DV
    row = lax.broadcasted_iota(jnp.int32, (L, L), 0)
    col = lax.broadcasted_iota(jnp.int32, (L, L), 1)
    causal = col <= row
    tri = jnp.where(causal, 1.0, 0.0).astype(BF16)
    lane = lax.broadcasted_iota(jnp.int32, (L, 128), 1)
    ones_col = jnp.where(lane == 0, 1.0, 0.0).astype(BF16)

    for bb in range(nbatch):
        G = gate_ref[bb] + bias_ref[...]
        LF = _log_sigmoid(G)
        hi = LF.astype(BF16)
        r1 = LF - hi.astype(F32)
        mid = r1.astype(BF16)
        lo = (r1 - mid.astype(F32)).astype(BF16)
        Bc = _dot(tri, hi) + _dot(tri, mid) + _dot(tri, lo)
        ZT = jnp.where(lane < H, G, Bc).T

        for h in range(H):
            q = qkvo_ref[bb, :, h * DK:(h + 1) * DK]
            k = qkvo_ref[bb, :, H * DK + h * DK: H * DK + (h + 1) * DK]
            v = qkvo_ref[bb, :, 2 * H * DK + h * DV: 2 * H * DK + (h + 1) * DV]
            o = qkvo_ref[bb, :, 3 * H * DK + h * DV: 3 * H * DK + (h + 1) * DV]
            i_col = G[:, h:h + 1]
            b_col = Bc[:, H + h:H + h + 1]
            i_row = ZT[h:h + 1, :]
            b_row = ZT[H + h:H + h + 1, :]
            m_prev = m_ref[bb * H + h]
            c_prev = c_ref[bb * H + h]

            a_col = b_col + m_prev
            D = jnp.where(causal, b_col - b_row + i_row, NEG_INF)
            m_t = jnp.maximum(a_col, jnp.max(D, axis=-1, keepdims=True))
            Dw = jnp.exp(D - m_t)
            aw = jnp.exp(a_col - m_t)
            s = (_dot_nt(q, k) * Dw).astype(BF16)
            v_aug = jnp.concatenate([v, ones_col], axis=-1)
            num = aw * _dot(q, c_prev.astype(BF16)) + _dot(s, v_aug)
            den = num[:, DV:DV + 1]
            hval = num[:, :DV] / jnp.maximum(jnp.abs(den), jnp.exp(-m_t))

            bL = b_col[L - 1:L, :]
            g_col = bL - b_col + i_col
            m_new = jnp.maximum(bL + m_prev, jnp.max(g_col, axis=0, keepdims=True))
            decay = jnp.exp(bL + m_prev - m_new)
            w_col = jnp.exp(g_col - m_new)
            wv = (w_col * v_aug.astype(F32)).astype(BF16)
            c_ref[bb * H + h] = decay * c_prev + _dot_tn(k, wv)
            m_ref[bb * H + h] = m_new

            ms = jnp.mean(hval * hval, axis=-1, keepdims=True)
            hn = hval * lax.rsqrt(ms + RMS_EPS) * ng_ref[:, h * DV:(h + 1) * DV]
            y_ref[bb, :, h * DV:(h + 1) * DV] = (jax.nn.sigmoid(o.astype(F32)) * hn).astype(BF16)


def _mlstm(qkvo, gates, bias_row, norm_g, B, S, L, nbatch):
    nc = S // L
    n_in = qkvo.shape[1]
    y = pl.pallas_call(
        functools.partial(_mlstm_kernel, L=L, nbatch=nbatch),
        grid=(B // nbatch, nc),
        in_specs=[pl.BlockSpec((nbatch, L, n_in), lambda b, c: (b, c, 0)),
                  pl.BlockSpec((nbatch, L, 128), lambda b, c: (b, c, 0)),
                  pl.BlockSpec((1, 128), lambda b, c: (0, 0)),
                  pl.BlockSpec((1, M_H * M_DV), lambda b, c: (0, 0))],
        out_specs=pl.BlockSpec((nbatch, L, M_H * M_DV), lambda b, c: (b, c, 0)),
        out_shape=jax.ShapeDtypeStruct((B, S, M_H * M_DV), BF16),
        scratch_shapes=[pltpu.VMEM((nbatch * M_H, M_DK, 2 * M_DV), F32),
                        pltpu.VMEM((nbatch * M_H, 1, 1), F32)],
        compiler_params=_cparams(("parallel", "arbitrary")),
        name="mlstm",
    )(qkvo.reshape(B, S, n_in), gates.reshape(B, S, 128), bias_row, norm_g)
    return y.reshape(B * S, M_H * M_DV)


def _mla_prep_kernel(lat_ref, pos_ref, cqg_ref, ckvg_ref, wq_ref, wqs_ref, wkv_ref,
                     qg_ref, qgs_ref, kg_ref, kgs_ref, freq_ref, sign_ref,
                     q_ref, k_ref, v_ref):
    H = A_H
    cq = lat_ref[:, :A_QR]
    ckv = lat_ref[:, A_QR:A_QR + A_KVR]
    kr = lat_ref[:, A_QR + A_KVR:A_QR + A_KVR + 128]
    krs = lat_ref[:, A_QR + A_KVR + 128:A_QR + A_KVR + 256]

    def rms(t, g):
        return t * lax.rsqrt(jnp.mean(t * t, axis=-1, keepdims=True) + RMS_EPS) * g

    cqn = rms(cq, cqg_ref[...]).astype(BF16)
    ckvn = rms(ckv, ckvg_ref[...]).astype(BF16)
    qf = _dot(cqn, wq_ref[...])
    qs = _dot(cqn, wqs_ref[...])
    kv = _dot(ckvn, wkv_ref[...])

    ang = pos_ref[...] * freq_ref[...]
    cos = jnp.cos(ang)
    sin = jnp.sin(ang) * sign_ref[...]
    scale = A_DQK ** -0.5
    kr_ss = jnp.sum(kr * kr, axis=-1, keepdims=True)

    for h in range(H):
        qh = qf[:, h * A_DPAD:(h + 1) * A_DPAD]
        rstd = lax.rsqrt(jnp.sum(qh * qh, axis=-1, keepdims=True) / A_DQK + RMS_EPS) * scale
        q_nope = qh[:, :128] * rstd * qg_ref[:, :128]
        q_rope = qh[:, 128:] * rstd * qg_ref[:, 128:]
        q_rope_s = qs[:, h * 128:(h + 1) * 128] * rstd * qgs_ref[...]
        q_ref[:, h * A_DPAD:h * A_DPAD + 128] = q_nope.astype(BF16)
        q_ref[:, h * A_DPAD + 128:(h + 1) * A_DPAD] = (q_rope * cos + q_rope_s * sin).astype(BF16)

        kn = kv[:, h * 256:h * 256 + 128]
        rstd_k = lax.rsqrt((jnp.sum(kn * kn, axis=-1, keepdims=True) + kr_ss) / A_DQK + RMS_EPS)
        k_nope = kn * rstd_k * kg_ref[:, :128]
        k_rope = kr * rstd_k * kg_ref[:, 128:]
        k_rope_s = krs * rstd_k * kgs_ref[...]
        k_ref[:, h * A_DPAD:h * A_DPAD + 128] = k_nope.astype(BF16)
        k_ref[:, h * A_DPAD + 128:(h + 1) * A_DPAD] = (k_rope * cos + k_rope_s * sin).astype(BF16)
        v_ref[:, h * A_DV:(h + 1) * A_DV] = kv[:, h * 256 + 128:(h + 1) * 256].astype(BF16)


def _mla_prep(lat, pos_col, consts, tm):
    T = lat.shape[0]
    full = lambda a: pl.BlockSpec(a.shape, lambda i: (0, 0))
    return pl.pallas_call(
        _mla_prep_kernel,
        grid=(T // tm,),
        in_specs=[pl.BlockSpec((tm, lat.shape[1]), lambda i: (i, 0)),
                  pl.BlockSpec((tm, 1), lambda i: (i, 0))] + [full(c) for c in consts],
        out_specs=[pl.BlockSpec((tm, A_H * A_DPAD), lambda i: (i, 0)),
                   pl.BlockSpec((tm, A_H * A_DPAD), lambda i: (i, 0)),
                   pl.BlockSpec((tm, A_H * A_DV), lambda i: (i, 0))],
        out_shape=[jax.ShapeDtypeStruct((T, A_H * A_DPAD), BF16),
                   jax.ShapeDtypeStruct((T, A_H * A_DPAD), BF16),
                   jax.ShapeDtypeStruct((T, A_H * A_DV), BF16)],
        compiler_params=_cparams(("parallel",)),
        name="mla_prep",
    )(lat, pos_col, *consts)


def _attn_kernel(q_ref, k_ref, v_ref, o_ref, *, tq, hp):
    qi = pl.program_id(2)

    def chunk(j, carry, masked):
        rows = pl.ds(pl.multiple_of(j * tq, tq), tq)
        out = []
        for h in range(hp):
            m, l, acc = carry[h]
            q = q_ref[:, h * A_DPAD:(h + 1) * A_DPAD]
            k = k_ref[rows, h * A_DPAD:(h + 1) * A_DPAD]
            v = v_ref[rows, h * A_DV:(h + 1) * A_DV]
            s = _dot_nt(q, k)
            if masked:
                row = lax.broadcasted_iota(jnp.int32, (tq, tq), 0)
                col = lax.broadcasted_iota(jnp.int32, (tq, tq), 1)
                s = jnp.where(col <= row, s, NEG_INF)
            m_new = jnp.maximum(m, jnp.max(s, axis=-1, keepdims=True))
            alpha = jnp.exp(m - m_new)
            p = jnp.exp(s - m_new)
            l = alpha * l + jnp.sum(p, axis=-1, keepdims=True)
            acc = alpha * acc + _dot(p.astype(BF16), v)
            out.append((m_new, l, acc))
        return tuple(out)

    init = tuple((jnp.full((tq, 1), NEG_INF, F32), jnp.zeros((tq, 1), F32), jnp.zeros((tq, A_DV), F32))
                 for _ in range(hp))
    carry = lax.fori_loop(0, qi, lambda j, c: chunk(j, c, False), init)
    carry = chunk(qi, carry, True)
    for h in range(hp):
        m, l, acc = carry[h]
        o_ref[:, h * A_DV:(h + 1) * A_DV] = (acc / l).astype(BF16)


def _mla_attn(q, k, v, B, S, tq, hp):
    T = B * S
    nq = S // tq
    return pl.pallas_call(
        functools.partial(_attn_kernel, tq=tq, hp=hp),
        grid=(B, A_H // hp, nq),
        in_specs=[pl.BlockSpec((tq, hp * A_DPAD), lambda b, h, i: (b * nq + i, h)),
                  pl.BlockSpec((S, hp * A_DPAD), lambda b, h, i: (b, h)),
                  pl.BlockSpec((S, hp * A_DV), lambda b, h, i: (b, h))],
        out_specs=pl.BlockSpec((tq, hp * A_DV), lambda b, h, i: (b * nq + i, h)),
        out_shape=jax.ShapeDtypeStruct((T, A_H * A_DV), BF16),
        compiler_params=_cparams(("parallel", "parallel", "arbitrary")),
        name="mla_attn",
    )(q, k, v)


def _outproj_kernel(x_ref, ym_ref, ya_ref, wom_ref, woa_ref, g_ref, wq_ref, k1_ref, k2_ref,
                    x1_ref, h2_ref, s1_ref, s2_ref):
    x1 = x_ref[...] + _dot(ym_ref[...], wom_ref[...]) + _dot(ya_ref[...], woa_ref[...])
    x1_ref[...] = x1
    ms = jnp.mean(x1 * x1, axis=-1, keepdims=True)
    h2f = x1 * lax.rsqrt(ms + RMS_EPS) * g_ref[...]
    h2 = h2f.astype(BF16)
    h2_ref[...] = h2
    qry = _dot(h2, wq_ref[...]).astype(BF16)
    half = P_DK // 2
    for h in range(P_H):
        s1_ref[h] = _dot_nt(k1_ref[...], qry[:, h * P_DK:h * P_DK + half])
        s2_ref[h] = _dot_nt(k2_ref[...], qry[:, h * P_DK + half:(h + 1) * P_DK])


def _out_proj(x2, ym, ya, wo_m, wo_a, g, wq, k1, k2, tm):
    T = x2.shape[0]
    full = lambda a: pl.BlockSpec(a.shape, lambda i: (0, 0))
    return pl.pallas_call(
        _outproj_kernel,
        grid=(T // tm,),
        in_specs=[pl.BlockSpec((tm, D_MODEL), lambda i: (i, 0)),
                  pl.BlockSpec((tm, ym.shape[1]), lambda i: (i, 0)),
                  pl.BlockSpec((tm, ya.shape[1]), lambda i: (i, 0)),
                  full(wo_m), full(wo_a), full(g), full(wq), full(k1), full(k2)],
        out_specs=[pl.BlockSpec((tm, D_MODEL), lambda i: (i, 0)),
                   pl.BlockSpec((tm, D_MODEL), lambda i: (i, 0)),
                   pl.BlockSpec((P_H, P_NK, tm), lambda i: (0, 0, i)),
                   pl.BlockSpec((P_H, P_NK, tm), lambda i: (0, 0, i))],
        out_shape=[jax.ShapeDtypeStruct((T, D_MODEL), F32),
                   jax.ShapeDtypeStruct((T, D_MODEL), BF16),
                   jax.ShapeDtypeStruct((P_H, P_NK, T), F32),
                   jax.ShapeDtypeStruct((P_H, P_NK, T), F32)],
        compiler_params=_cparams(("parallel",)),
        name="out_peerq",
    )(x2, ym, ya, wo_m, wo_a, g, wq, k1, k2)


def _top16(x):
    vals = []
    for _ in range(P_TOPK):
        m = jnp.max(x, axis=0, keepdims=True)
        vals.append(m)
        x = jnp.where(x == m, NEG_INF, x)
    return vals


def _oddeven_merge(lo, hi, r):
    step = r * 2
    if step < hi - lo:
        yield from _oddeven_merge(lo, hi, step)
        yield from _oddeven_merge(lo + r, hi, step)
        yield from [(i, i + r) for i in range(lo + r, hi - r, step)]
    else:
        yield (lo, lo + r)


def _oddeven_merge_sort(lo, hi):
    if hi - lo >= 1:
        mid = lo + (hi - lo) // 2
        yield from _oddeven_merge_sort(lo, mid)
        yield from _oddeven_merge_sort(mid + 1, hi)
        yield from _oddeven_merge(lo, hi, 1)


_SORT16 = tuple(_oddeven_merge_sort(0, P_TOPK - 1))
_SUBLANES = 8


def _cmpx(v, i, j):
    a, b = v[i], v[j]
    v[i] = jnp.maximum(a, b)
    v[j] = jnp.minimum(a, b)


def _top16_sorted(x):
    v = [x[i * _SUBLANES:(i + 1) * _SUBLANES, :] for i in range(P_TOPK)]
    for i, j in _SORT16:
        _cmpx(v, i, j)
    for shift in (4, 2, 1):
        o = [pltpu.roll(t, shift, axis=0) for t in v]
        v = [jnp.maximum(v[i], o[P_TOPK - 1 - i]) for i in range(P_TOPK)]
        for d in (8, 4, 2, 1):
            for i in range(P_TOPK):
                if not i & d:
                    _cmpx(v, i, i + d)
    return v


_CAND = [(a, b) for a in range(P_TOPK) for b in range(P_TOPK) if (a + 1) * (b + 1) <= P_TOPK]


_ROUTE_LANES = 256


def _route_kernel(s1_ref, s2_ref, r2_ref, e2_ref, cnt_ref, f_ref):
    ng = P_NK // _SUBLANES
    for c in range(s1_ref.shape[2] // _ROUTE_LANES):
        lanes = slice(c * _ROUTE_LANES, (c + 1) * _ROUTE_LANES)
        s1 = s1_ref[0, :, lanes]
        s2 = s2_ref[0, :, lanes]
        v1s = _top16_sorted(s1)
        v2s = _top16_sorted(s2)
        v1 = [t[0:1, :] for t in v1s]
        v2 = [t[0:1, :] for t in v2s]
        cand = jnp.concatenate([v1[a] + v2[b] for a, b in _CAND], axis=0)
        tau = _top16(cand)[P_TOPK - 1]
        top = v1[0] + v2[0]
        z = jnp.zeros_like(tau)
        cnt_a = [jnp.zeros_like(tau) for _ in range(P_TOPK)]
        for a, b in _CAND:
            sm = v1[a] + v2[b]
            sel = sm >= tau
            z = z + jnp.where(sel, jnp.exp(sm - top), 0.0)
            cnt_a[a] = cnt_a[a] + jnp.where(sel, 1.0, 0.0)
        cnt_b = [jnp.broadcast_to(t, (_SUBLANES, t.shape[1])) for t in cnt_a]
        for i in range(ng):
            rows = slice(i * _SUBLANES, (i + 1) * _SUBLANES)
            x1, x2 = s1[rows, :], s2[rows, :]
            r2 = jnp.full_like(x2, float(P_TOPK))
            cnt = jnp.zeros_like(x1)
            for a in reversed(range(P_TOPK)):
                r2 = jnp.where(x2 >= v2s[a], float(a), r2)
                cnt = jnp.where(x1 == v1s[a], cnt_b[a], cnt)
            r2_ref[0, rows, lanes] = r2
            cnt_ref[0, rows, lanes] = cnt
        e2_ref[0, :, lanes] = jnp.exp(s2 - v2[0])
        f_ref[0, :, lanes] = jnp.exp(s1 - v1[0]) / z


def _peer_route(s1t, s2t, tl):
    T = s1t.shape[2]
    spec = pl.BlockSpec((1, P_NK, tl), lambda h, i: (h, 0, i))
    shp32 = jax.ShapeDtypeStruct((P_H, P_NK, T), F32)
    return pl.pallas_call(
        _route_kernel,
        grid=(P_H, T // tl),
        in_specs=[spec, spec],
        out_specs=[spec, spec, spec, spec],
        out_shape=[shp32, shp32, shp32, shp32],
        compiler_params=_cparams(("parallel", "parallel")),
        name="peer_route",
    )(s1t, s2t)


def _gelu2(x):
    return x * (1.0 + lax.erf(x * (2.0 ** -0.5)))


_LANES = 128
_PACK = 16
_N1_GROUP = 1


def _peer_kernel(h2_ref, x1_ref, u_ref, v_ref, r2f_ref, e2f_ref, cnt_ref, f_ref, o_ref,
                 acc_ref, a_ref, p_ref, r2_ref, e2_ref, *, nb, tm):
    e = pl.program_id(1)

    @pl.when(e == 0)
    def _():
        acc_ref[...] = jnp.zeros_like(acc_ref)
        for h in range(P_H):
            r2_ref[h] = r2f_ref[h].astype(BF16)
            e2_ref[h] = e2f_ref[h].astype(BF16)

    a_ref[...] = _dot_nt(u_ref[...], h2_ref[...])
    n1_0 = pl.multiple_of(e * nb, nb)
    zero = jnp.zeros((_PACK, _LANES), BF16)
    for g in range(tm // _LANES):
        lanes = slice(g * _LANES, (g + 1) * _LANES)
        for jc in range(nb // _N1_GROUP):
            w = [[None] * (P_NK // _PACK) for _ in range(_N1_GROUP)]
            for h in range(P_H):
                cnt8 = cnt_ref[h, pl.ds(n1_0, nb), lanes]
                f8 = 0.5 * f_ref[h, pl.ds(n1_0, nb), lanes]
                r2t = [r2_ref[h, r * _PACK:(r + 1) * _PACK, lanes] for r in range(P_NK // _PACK)]
                e2t = [e2_ref[h, r * _PACK:(r + 1) * _PACK, lanes] for r in range(P_NK // _PACK)]
                for jj in range(_N1_GROUP):
                    j = jc * _N1_GROUP + jj
                    cb = jnp.broadcast_to(cnt8[j:j + 1, :], (_PACK, _LANES)).astype(BF16)
                    fb = jnp.broadcast_to(f8[j:j + 1, :], (_PACK, _LANES)).astype(BF16)
                    for r in range(P_NK // _PACK):
                        t = jnp.where(r2t[r] < cb, e2t[r], zero) * fb
                        w[jj][r] = t if w[jj][r] is None else w[jj][r] + t
            for jj in range(_N1_GROUP):
                j = jc * _N1_GROUP + jj
                for r in range(P_NK // _PACK):
                    rows = slice(j * P_NK + r * _PACK, j * P_NK + (r + 1) * _PACK)
                    p_ref[rows, lanes] = w[jj][r] * _gelu2(a_ref[rows, lanes]).astype(BF16)
    acc_ref[...] += _dot_tn(p_ref[...], v_ref[...])

    @pl.when(e == pl.num_programs(1) - 1)
    def _():
        o_ref[...] = x1_ref[...] + acc_ref[...]


def _peer_mlp(h2, x1, u, v, r2t, e2t, cntt, ft, tm, nb):
    T = x1.shape[0]
    ne = P_NK // nb
    te = nb * P_NK
    rspec = pl.BlockSpec((P_H, P_NK, tm), lambda i, e: (0, 0, i))
    return pl.pallas_call(
        functools.partial(_peer_kernel, nb=nb, tm=tm),
        grid=(T // tm, ne),
        in_specs=[pl.BlockSpec((tm, D_MODEL), lambda i, e: (i, 0)),
                  pl.BlockSpec((tm, D_MODEL), lambda i, e: (i, 0)),
                  pl.BlockSpec((te, D_MODEL), lambda i, e: (e, 0)),
                  pl.BlockSpec((te, D_MODEL), lambda i, e: (e, 0)),
                  rspec, rspec, rspec, rspec],
        out_specs=pl.BlockSpec((tm, D_MODEL), lambda i, e: (i, 0)),
        out_shape=jax.ShapeDtypeStruct((T, D_MODEL), F32),
        scratch_shapes=[pltpu.VMEM((tm, D_MODEL), F32),
                        pltpu.VMEM((te, tm), F32),
                        pltpu.VMEM((te, tm), BF16),
                        pltpu.VMEM((P_H, P_NK, tm), BF16),
                        pltpu.VMEM((P_H, P_NK, tm), BF16)],
        compiler_params=_cparams(("parallel", "arbitrary")),
        name="peer_mlp",
    )(h2, x1, u, v, r2t, e2t, cntt, ft)


def _swap_halves(t, axis=-1):
    a, b = jnp.split(t, 2, axis=axis)
    return jnp.concatenate([b, a], axis=axis)


def _pad_cols(t, n):
    return jnp.pad(t, ((0, 0), (0, n - t.shape[1])))


def _layer(x2, pos_col, B, S, attn_norm_g, w_in, b_igate, b_fgate, mlstm_norm_g, cq_norm_g, w_uq,
           ckv_norm_g, w_ukv, q_norm_g, k_norm_g, w_out, ffn_norm_g, peer_w_q, peer_keys_1,
           peer_keys_2, peer_u, peer_v):
    T = B * S
    n_qkvo = 4 * M_H * M_DK
    o_gate = n_qkvo
    o_cq = o_gate + 2 * M_H
    o_ckv = o_cq + A_QR
    o_kr = o_ckv + A_KVR

    w_main = w_in[:, :n_qkvo].astype(BF16)
    w_kr = w_in[:, o_kr:o_kr + A_ROPE]
    w_lat = jnp.concatenate([w_in[:, o_cq:o_kr], _pad_cols(w_kr, 128),
                             _pad_cols(_swap_halves(w_kr), 128)], axis=1).astype(BF16)
    w_gate = _pad_cols(w_in[:, o_gate:o_gate + 2 * M_H], 128).astype(BF16)
    bias_row = _pad_cols(jnp.concatenate([b_igate, b_fgate])[None, :], 128)

    wq_h = w_uq.reshape(A_QR, A_H, A_DQK)
    wq_pad = jnp.pad(wq_h, ((0, 0), (0, 0), (0, A_DPAD - A_DQK))).reshape(A_QR, A_H * A_DPAD).astype(BF16)
    wq_sw = jnp.pad(_swap_halves(wq_h[:, :, A_NOPE:]), ((0, 0), (0, 0), (0, 128 - A_ROPE)))
    wq_sw = wq_sw.reshape(A_QR, A_H * 128).astype(BF16)
    w_kv = w_ukv.astype(BF16)
    qg = _pad_cols(q_norm_g[None, :], A_DPAD)
    qgs = _pad_cols(_swap_halves(q_norm_g[A_NOPE:])[None, :], 128)
    kg = _pad_cols(k_norm_g[None, :], A_DPAD)
    kgs = _pad_cols(_swap_halves(k_norm_g[A_NOPE:])[None, :], 128)
    inv_freq = ROPE_THETA ** (-jnp.arange(0, A_ROPE, 2, dtype=F32) / A_ROPE)
    freq_row = _pad_cols(jnp.concatenate([inv_freq, inv_freq])[None, :], 128)
    half = A_ROPE // 2
    sign_row = _pad_cols(jnp.concatenate([-jnp.ones((half,), F32), jnp.ones((half,), F32)])[None, :], 128)

    qkvo, lat, gates = _in_proj(x2, attn_norm_g[None, :], w_main, w_lat, w_gate, tm=min(512, T))
    y_m = _mlstm(qkvo, gates, bias_row, mlstm_norm_g[None, :], B, S, L=min(256, S), nbatch=1)
    consts = [cq_norm_g[None, :], ckv_norm_g[None, :], wq_pad, wq_sw, w_kv, qg, qgs, kg, kgs,
              freq_row, sign_row]
    q, k, v = _mla_prep(lat, pos_col, consts, tm=min(512, T))
    y_a = _mla_attn(q, k, v, B, S, tq=min(1024, S), hp=2)
    wo = w_out.astype(BF16)
    x1, h2, s1t, s2t = _out_proj(x2, y_m, y_a, wo[:M_H * M_DV], wo[M_H * M_DV:], ffn_norm_g[None, :],
                                  peer_w_q.astype(BF16), peer_keys_1.astype(BF16),
                                  peer_keys_2.astype(BF16), tm=min(512, T))
    r2t, e2t, cntt, ft = _peer_route(s1t, s2t, tl=min(1024, T))
    return _peer_mlp(h2, x1, peer_u.astype(BF16), peer_v.astype(BF16), r2t, e2t, cntt, ft,
                     tm=min(512, T), nb=8)


def kernel(x, positions, attn_norm_g, w_in, b_igate, b_fgate, mlstm_norm_g, cq_norm_g, w_uq, ckv_norm_g,
           w_ukv, q_norm_g, k_norm_g, w_out, ffn_norm_g, peer_w_q, peer_keys_1, peer_keys_2, peer_u, peer_v):
    B, S, D = x.shape
    x2 = x.reshape(B * S, D)
    pos_col = positions.reshape(B * S, 1).astype(F32)
    for l in range(attn_norm_g.shape[0]):
        x2 = _layer(x2, pos_col, B, S, attn_norm_g[l], w_in[l], b_igate[l], b_fgate[l], mlstm_norm_g[l],
                    cq_norm_g[l], w_uq[l], ckv_norm_g[l], w_ukv[l], q_norm_g[l], k_norm_g[l], w_out[l],
                    ffn_norm_g[l], peer_w_q[l], peer_keys_1[l], peer_keys_2[l], peer_u[l], peer_v[l])
    return x2.reshape(B, S, D)
```
